```python
import jax, jax.numpy as jnp
from jax import lax
import numpy as np

D_MODEL = 1024
BATCH = 4
SEQ = 4096
DEPTH = 2

RET_HEADS = 4
RET_DK = 128
RET_DV = 256
RET_CHUNK = 128
RET_QK_W = RET_HEADS * RET_DK
RET_V_W = RET_HEADS * RET_DV
DN_HEADS = 8
DN_DK = 128
DN_DV = 128
DN_CHUNK = 64
DN_QK_W = DN_HEADS * DN_DK
DN_V_W = DN_HEADS * DN_DV
CONV_K = 4
CONV_W = 2 * DN_QK_W + DN_V_W
ROPE_BASE = 10000.0
EPS = 1e-6

SPLITS = (RET_QK_W, RET_QK_W, RET_V_W, RET_V_W,
          DN_QK_W, DN_QK_W, DN_V_W, DN_V_W, DN_HEADS, DN_HEADS,
          D_MODEL, D_MODEL)
IN_W = sum(SPLITS)

kernel_name = "hybrid_retention_gated_deltanet_parallel"


def _split_points():
    pts, acc = [], 0
    for s in SPLITS[:-1]:
        acc += s
        pts.append(acc)
    return pts


def rmsnorm(x, g):
    xf = x.astype(jnp.float32)
    y = xf * lax.rsqrt(jnp.mean(xf * xf, axis=-1, keepdims=True) + EPS) * g.astype(jnp.float32)
    return y.astype(x.dtype)


def rope(x, pos):
    half = x.shape[-1] // 2
    inv_freq = jnp.power(ROPE_BASE, -jnp.arange(half, dtype=jnp.float32) / half)
    ang = pos.astype(jnp.float32)[:, None] * inv_freq[None, :]
    cos = jnp.cos(ang)[None, :, None, :]
    sin = jnp.sin(ang)[None, :, None, :]
    x1, x2 = x[..., :half], x[..., half:]
    return jnp.concatenate([x1 * cos - x2 * sin, x2 * cos + x1 * sin], axis=-1)


def l2norm(x):
    return x * lax.rsqrt(jnp.sum(x * x, axis=-1, keepdims=True) + EPS)


def causal_conv(x, w):
    c = x.shape[-1]
    return lax.conv_general_dilated(x, w[:, None, :].astype(x.dtype), window_strides=(1,),
                                    padding=[(CONV_K - 1, 0)],
                                    dimension_numbers=('NWC', 'WIO', 'NWC'),
                                    feature_group_count=c)


def retention(q, k, v):
    B, S, H, DK = q.shape
    DV = v.shape[-1]
    C = RET_CHUNK
    N = S // C
    log_g = jnp.log1p(-jnp.power(2.0, -5.0 - jnp.arange(H, dtype=jnp.float32)))
    idx = jnp.arange(C, dtype=jnp.float32)
    rel = idx[:, None] - idx[None, :]
    intra = jnp.where(rel[None] >= 0,
                      jnp.exp(jnp.maximum(rel, 0.0)[None] * log_g[:, None, None]), 0.0)
    q = q.reshape(B, N, C, H, DK)
    k = k.reshape(B, N, C, H, DK)
    v = v.reshape(B, N, C, H, DV)
    scores = jnp.einsum('bnihd,bnjhd->bnhij', q, k) * intra
    inner = jnp.einsum('bnhij,bnjhe->bnihe', scores, v)
    k_dec = k * jnp.exp((C - 1 - idx)[:, None] * log_g[None, :])[:, :, None]
    kv = jnp.einsum('bnjhd,bnjhe->nbhde', k_dec, v)
    chunk_decay = jnp.exp(C * log_g)[:, None, None]

    def step(state, kv_n):
        return state * chunk_decay + kv_n, state

    _, state_start = lax.scan(step, jnp.zeros((B, H, DK, DV), jnp.float32), kv)
    q_dec = q * jnp.exp((idx + 1.0)[:, None] * log_g[None, :])[:, :, None]
    cross = jnp.einsum('bnihd,nbhde->bnihe', q_dec, state_start)
    return (inner + cross).reshape(B, S, H, DV)


def gated_deltanet(q, k, v, beta, g):
    B, S, H, DK = q.shape
    DV = v.shape[-1]
    C = DN_CHUNK
    N = S // C
    to_chunks = lambda t: t.transpose(0, 2, 1, 3).reshape(B, H, N, C, t.shape[-1])
    q, k, v = to_chunks(q), to_chunks(k), to_chunks(v)
    beta = beta.transpose(0, 2, 1).reshape(B, H, N, C)
    gc = jnp.cumsum(g.transpose(0, 2, 1).reshape(B, H, N, C), axis=-1)
    idx = jnp.arange(C)
    causal = idx[:, None] >= idx[None, :]
    strict = idx[:, None] > idx[None, :]
    diff = gc[..., :, None] - gc[..., None, :]
    decay = jnp.exp(jnp.where(causal, diff, -jnp.inf))
    k_beta = k * beta[..., None]
    L = jnp.where(strict, jnp.einsum('bhncd,bhnjd->bhncj', k_beta, k) * decay, 0.0)
    a = L + jnp.eye(C, dtype=jnp.float32)
    rhs = jnp.concatenate([v * beta[..., None], k_beta * jnp.exp(gc)[..., None]], axis=-1)
    sol = lax.linalg.triangular_solve(a, rhs, left_side=True, lower=True)
    u, w = sol[..., :DV], sol[..., DV:]
    attn = jnp.einsum('bhncd,bhnjd->bhncj', q, k) * decay
    q_dec = q * jnp.exp(gc)[..., None]
    g_last = gc[..., -1]
    k_dec = k * jnp.exp(g_last[..., None] - gc)[..., None]
    front = lambda t: jnp.moveaxis(t, 2, 0)
    xs = (front(q_dec), front(k_dec), front(u), front(w), front(attn), front(g_last))

    def step(state, inp):
        q_n, k_n, u_n, w_n, attn_n, gl = inp
        v_new = u_n - jnp.einsum('bhcd,bhde->bhce', w_n, state)
        o = jnp.einsum('bhcd,bhde->bhce', q_n, state) + jnp.einsum('bhcj,bhje->bhce', attn_n, v_new)
        state = state * jnp.exp(gl)[..., None, None] + jnp.einsum('bhcd,bhce->bhde', k_n, v_new)
        return state, o

    _, o = lax.scan(step, jnp.zeros((B, H, DK, DV), jnp.float32), xs)
    return o.transpose(1, 0, 3, 2, 4).reshape(B, S, H, DV)


def hybrid_layer(x, norm_g, w_in, conv_w, a_log, dt_bias, dn_g, w_up_r, w_up_d, w_o):
    B, S, _ = x.shape
    f32 = jnp.float32
    h = rmsnorm(x, norm_g)
    proj = h @ w_in
    (rq, rk, rv, rz, dq, dk, dv, dz, db, da, gr, gd) = jnp.split(proj, _split_points(), axis=-1)

    pos = jnp.arange(S)
    rq = rope(rq.reshape(B, S, RET_HEADS, RET_DK).astype(f32), pos)
    rk = rope(rk.reshape(B, S, RET_HEADS, RET_DK).astype(f32), pos) * (RET_DK ** -0.5)
    rv = rv.reshape(B, S, RET_HEADS, RET_DV).astype(f32)
    o_r = retention(rq, rk, rv)
    mu = jnp.mean(o_r, axis=-1, keepdims=True)
    var = jnp.mean(jnp.square(o_r - mu), axis=-1, keepdims=True)
    o_r = ((o_r - mu) * lax.rsqrt(var + EPS)).reshape(B, S, RET_V_W).astype(x.dtype)
    o_r = o_r * jax.nn.silu(rz)

    qkv = jax.nn.silu(causal_conv(jnp.concatenate([dq, dk, dv], axis=-1), conv_w))
    dq, dk, dv = jnp.split(qkv, [DN_QK_W, 2 * DN_QK_W], axis=-1)
    dq = l2norm(dq.reshape(B, S, DN_HEADS, DN_DK).astype(f32)) * (DN_DK ** -0.5)
    dk = l2norm(dk.reshape(B, S, DN_HEADS, DN_DK).astype(f32))
    dv = dv.reshape(B, S, DN_HEADS, DN_DV).astype(f32)
    beta = jax.nn.sigmoid(db.astype(f32))
    g = -jnp.exp(a_log.astype(f32)) * jax.nn.softplus(da.astype(f32) + dt_bias.astype(f32))
    o_d = gated_deltanet(dq, dk, dv, beta, g)
    o_d = o_d * lax.rsqrt(jnp.mean(o_d * o_d, axis=-1, keepdims=True) + EPS) * dn_g.astype(f32)
    o_d = o_d.reshape(B, S, DN_V_W).astype(x.dtype) * jax.nn.silu(dz)

    y = jax.nn.sigmoid(gr) * (o_r @ w_up_r) + jax.nn.sigmoid(gd) * (o_d @ w_up_d)
    return x + y @ w_o


def setup_inputs(seed: int = 0) -> dict:
    key = jax.random.key(seed)
    ks = jax.random.split(key, 12)
    f32 = jnp.float32
    x = jax.random.normal(ks[0], (BATCH, SEQ, D_MODEL), f32)
    norm_gain = 1.0 + 0.02 * jax.random.normal(ks[1], (DEPTH, D_MODEL), f32)
    w_in = jax.random.normal(ks[2], (DEPTH, D_MODEL, IN_W), f32) * D_MODEL ** -0.5
    conv_w = jax.random.normal(ks[3], (DEPTH, CONV_K, CONV_W), f32) * CONV_K ** -0.5
    a_log = jnp.log(jax.random.uniform(ks[4], (DEPTH, DN_HEADS), f32, 1.0, 16.0))
    dt = jnp.exp(jax.random.uniform(ks[5], (DEPTH, DN_HEADS), f32, np.log(1e-3), np.log(1e-1)))
    dt_bias = dt + jnp.log(-jnp.expm1(-dt))
    dn_norm_gain = 1.0 + 0.02 * jax.random.normal(ks[6], (DEPTH, DN_DV), f32)
    w_up_ret = jax.random.normal(ks[7], (DEPTH, RET_V_W, D_MODEL), f32) * RET_V_W ** -0.5
    w_up_dn = jax.random.normal(ks[8], (DEPTH, DN_V_W, D_MODEL), f32) * DN_V_W ** -0.5
    w_out = jax.random.normal(ks[9], (DEPTH, D_MODEL, D_MODEL), f32) * D_MODEL ** -0.5
    final_gain = 1.0 + 0.02 * jax.random.normal(ks[10], (D_MODEL,), f32)
    return {"x": x, "norm_gain": norm_gain, "w_in": w_in, "conv_w": conv_w, "a_log": a_log,
            "dt_bias": dt_bias, "dn_norm_gain": dn_norm_gain, "w_up_ret": w_up_ret,
            "w_up_dn": w_up_dn, "w_out": w_out, "final_gain": final_gain}


def reference(x, norm_gain, w_in, conv_w, a_log, dt_bias, dn_norm_gain, w_up_ret, w_up_dn,
              w_out, final_gain):
    for l in range(DEPTH):
        x = hybrid_layer(x, norm_gain[l], w_in[l], conv_w[l], a_log[l], dt_bias[l],
                         dn_norm_gain[l], w_up_ret[l], w_up_dn[l], w_out[l])
    return rmsnorm(x, final_gain)
```

```python
import functools

import jax
import jax.numpy as jnp
from jax import lax
from jax.experimental import pallas as pl
from jax.experimental.pallas import tpu as pltpu

D_MODEL = 1024
DEPTH = 2
RET_HEADS = 4
RET_DK = 128
RET_DV = 256
DN_HEADS = 8
DN_DK = 128
DN_DV = 128
DN_CHUNK = 64
CONV_K = 4
ROPE_BASE = 10000.0
EPS = 1e-6

LANES = 128
SUBLANES = 8
TILE = 256
N_CHUNKS = TILE // DN_CHUNK

C_RQ = 0
C_RK = C_RQ + RET_HEADS * RET_DK
C_RV = C_RK + RET_HEADS * RET_DK
C_RZ = C_RV + RET_HEADS * RET_DV
C_DQ = C_RZ + RET_HEADS * RET_DV
C_DK = C_DQ + DN_HEADS * DN_DK
C_DV = C_DK + DN_HEADS * DN_DK
C_DZ = C_DV + DN_HEADS * DN_DV
C_GR = C_DZ + DN_HEADS * DN_DV
C_GD = C_GR + D_MODEL
C_BD = C_GD + D_MODEL
N_PROJ = C_BD + LANES
PROJ_CHUNK = 512
A_LANE = DN_HEADS

VMEM_LIMIT_BYTES = 56 * 1024 * 1024

_F32 = jnp.float32
_BF16 = jnp.bfloat16


def _dot(a, b):
    return jnp.dot(a.astype(_BF16), b.astype(_BF16), preferred_element_type=_F32)


def _dot_nt(a, b):
    return lax.dot_general(a.astype(_BF16), b.astype(_BF16), (((1,), (1,)), ((), ())),
                           preferred_element_type=_F32)


def _dot_tn(a, b):
    return lax.dot_general(a.astype(_BF16), b.astype(_BF16), (((0,), (0,)), ((), ())),
                           preferred_element_type=_F32)


def _silu(x):
    return x * jax.nn.sigmoid(x)


def _softplus(x):
    return jnp.maximum(x, 0.0) + jnp.log1p(jnp.exp(-jnp.abs(x)))


def _layer_kernel(x_ref, ng_ref, win_ref, convw_ref, alog_ref, dtb_ref, dng_ref, wur_ref, wud_ref,
                  wo_ref, fg_ref, cos_ref, sin_ref, dmat_ref, qdec_ref, kdec_ref, cdec_ref,
                  out_ref, proj_ref, rstate_ref, dstate_ref, *, final_norm):
    rows = pl.ds(SUBLANES, TILE)

    @pl.when(pl.program_id(1) == 0)
    def _():
        proj_ref[0:SUBLANES, :] = jnp.zeros((SUBLANES, N_PROJ), _F32)
        rstate_ref[...] = jnp.zeros_like(rstate_ref)
        dstate_ref[...] = jnp.zeros_like(dstate_ref)

    x = x_ref[...]
    h = x * lax.rsqrt(jnp.mean(x * x, axis=-1, keepdims=True) + EPS) * ng_ref[...]
    hb = h.astype(_BF16)
    for c0 in range(0, N_PROJ, PROJ_CHUNK):
        wdt = min(PROJ_CHUNK, N_PROJ - c0)
        proj_ref[rows, c0:c0 + wdt] = jnp.dot(hb, win_ref[:, c0:c0 + wdt], preferred_element_type=_F32)

    cos = cos_ref[...]
    sin = sin_ref[...]
    o_r = []
    for hh in range(RET_HEADS):
        q = proj_ref[rows, C_RQ + hh * RET_DK:C_RQ + (hh + 1) * RET_DK]
        k = proj_ref[rows, C_RK + hh * RET_DK:C_RK + (hh + 1) * RET_DK]
        v = proj_ref[rows, C_RV + hh * RET_DV:C_RV + (hh + 1) * RET_DV]
        z = proj_ref[rows, C_RZ + hh * RET_DV:C_RZ + (hh + 1) * RET_DV]
        q = q * cos + pltpu.roll(q, RET_DK // 2, 1) * sin
        k = (k * cos + pltpu.roll(k, RET_DK // 2, 1) * sin) * (RET_DK ** -0.5)
        scores = _dot_nt(q, k) * dmat_ref[hh]
        state = rstate_ref[hh]
        o = _dot(scores, v) + _dot(q * qdec_ref[hh], state)
        rstate_ref[hh] = state * cdec_ref[hh] + _dot_tn(k * kdec_ref[hh], v)
        mu = jnp.mean(o, axis=-1, keepdims=True)
        d = o - mu
        var = jnp.mean(d * d, axis=-1, keepdims=True)
        o_r.append((d * lax.rsqrt(var + EPS) * _silu(z)).astype(_BF16))
    o_r = jnp.concatenate(o_r, axis=1)

    bd = proj_ref[rows, C_BD:C_BD + LANES]
    beta_all = jax.nn.sigmoid(bd)
    g_all = -jnp.exp(alog_ref[...]) * _softplus(bd + dtb_ref[...])
    row_in_chunk = lax.broadcasted_iota(jnp.int32, (TILE, LANES), 0) & (DN_CHUNK - 1)
    gc_all = g_all
    shift = 1
    while shift < DN_CHUNK:
        gc_all = gc_all + jnp.where(row_in_chunk >= shift, pltpu.roll(gc_all, shift, 0), 0.0)
        shift *= 2
    gl_all = jnp.broadcast_to(
        gc_all.reshape(N_CHUNKS, DN_CHUNK, LANES)[:, DN_CHUNK - 1:DN_CHUNK, :],
        (N_CHUNKS, DN_CHUNK, LANES)).reshape(TILE, LANES)
    gc_t = gc_all.T

    ri = lax.broadcasted_iota(jnp.int32, (TILE, TILE), 0)
    ci = lax.broadcasted_iota(jnp.int32, (TILE, TILE), 1)
    same_chunk = (ri // DN_CHUNK) == (ci // DN_CHUNK)
    causal = same_chunk & (ri >= ci)
    strict = same_chunk & (ri > ci)
    eye = (ri == ci).astype(_F32)
    cw = convw_ref[...]

    def conv_silu(c0):
        y = (cw[3:4, c0:c0 + LANES] * proj_ref[pl.ds(SUBLANES, TILE), C_DQ + c0:C_DQ + c0 + LANES]
             + cw[2:3, c0:c0 + LANES] * proj_ref[pl.ds(SUBLANES - 1, TILE), C_DQ + c0:C_DQ + c0 + LANES]
             + cw[1:2, c0:c0 + LANES] * proj_ref[pl.ds(SUBLANES - 2, TILE), C_DQ + c0:C_DQ + c0 + LANES]
             + cw[0:1, c0:c0 + LANES] * proj_ref[pl.ds(SUBLANES - 3, TILE), C_DQ + c0:C_DQ + c0 + LANES])
        return _silu(y)

    o_d = []
    for hh in range(DN_HEADS):
        q = conv_silu(hh * DN_DK)
        k = conv_silu(C_DK - C_DQ + hh * DN_DK)
        v = conv_silu(C_DV - C_DQ + hh * DN_DV)
        z = proj_ref[rows, C_DZ + hh * DN_DV:C_DZ + (hh + 1) * DN_DV]
        q = q * lax.rsqrt(jnp.sum(q * q, axis=-1, keepdims=True) + EPS) * (DN_DK ** -0.5)
        k = k * lax.rsqrt(jnp.sum(k * k, axis=-1, keepdims=True) + EPS)
        beta = beta_all[:, hh:hh + 1]
        gc = gc_all[:, A_LANE + hh:A_LANE + hh + 1]
        gl = gl_all[:, A_LANE + hh:A_LANE + hh + 1]
        gc_row = gc_t[A_LANE + hh:A_LANE + hh + 1, :]
        decay = jnp.where(causal, jnp.exp(jnp.where(causal, gc - gc_row, 0.0)), 0.0)
        k_beta = k * beta
        gram = _dot_nt(jnp.concatenate([k_beta, q], axis=0), k)
        lmat = jnp.where(strict, gram[:TILE] * decay, 0.0)
        attn = gram[TILE:] * decay
        npow = -lmat
        tinv = eye + npow
        step = 1
        while step < DN_CHUNK // 2:
            npow = _dot(npow, npow)
            tinv = tinv + _dot(tinv, npow)
            step *= 2
        e_gc = jnp.exp(gc)
        sol = _dot(tinv, jnp.concatenate([v * beta, k_beta * e_gc], axis=1))
        u = sol[:, :DN_DV]
        w = sol[:, DN_DV:]
        q_dec = q * e_gc
        k_dec = k * jnp.exp(gl - gc)
        e_gl = jnp.exp(gl)
        state = dstate_ref[hh]
        outs = []
        for c in range(N_CHUNKS):
            r0 = c * DN_CHUNK
            ws = _dot(jnp.concatenate([w[r0:r0 + DN_CHUNK], q_dec[r0:r0 + DN_CHUNK]], axis=0), state)
            v_new = u[r0:r0 + DN_CHUNK] - ws[:DN_CHUNK]
            outs.append(ws[DN_CHUNK:] + _dot(attn[r0:r0 + DN_CHUNK, r0:r0 + DN_CHUNK], v_new))
            state = state * e_gl[r0:r0 + 1, :] + _dot_tn(k_dec[r0:r0 + DN_CHUNK], v_new)
        dstate_ref[hh] = state
        o = jnp.concatenate(outs, axis=0)
        o = o * lax.rsqrt(jnp.mean(o * o, axis=-1, keepdims=True) + EPS) * dng_ref[...]
        o_d.append((o * _silu(z)).astype(_BF16))
    o_d = jnp.concatenate(o_d, axis=1)

    proj_ref[0:SUBLANES, C_DQ:C_DZ] = proj_ref[TILE:TILE + SUBLANES, C_DQ:C_DZ]

    gate_r = jax.nn.sigmoid(proj_ref[rows, C_GR:C_GR + D_MODEL])
    gate_d = jax.nn.sigmoid(proj_ref[rows, C_GD:C_GD + D_MODEL])
    y = (gate_r * jnp.dot(o_r, wur_ref[...], preferred_element_type=_F32)
         + gate_d * jnp.dot(o_d, wud_ref[...], preferred_element_type=_F32))
    out = x + jnp.dot(y.astype(_BF16), wo_ref[...], preferred_element_type=_F32)
    if final_norm:
        out = out * lax.rsqrt(jnp.mean(out * out, axis=-1, keepdims=True) + EPS) * fg_ref[...]
    out_ref[...] = out


def _const_spec(shape):
    zeros = (0,) * len(shape)
    return pl.BlockSpec(shape, lambda b, t: zeros, pipeline_mode=pl.Buffered(1))


def _layer(x, ng, win, convw, alog, dtb, dng, wur, wud, wo, fg, tables, *, final_norm):
    batch, seq, _ = x.shape
    cos, sin, dmat, qdec, kdec, cdec = tables
    consts = (ng, win, convw, alog, dtb, dng, wur, wud, wo, fg)
    tile_spec = pl.BlockSpec((None, TILE, D_MODEL), lambda b, t: (b, t, 0))
    rope_spec = pl.BlockSpec((TILE, LANES), lambda b, t: (t, 0))
    return pl.pallas_call(
        functools.partial(_layer_kernel, final_norm=final_norm),
        grid=(batch, seq // TILE),
        in_specs=[tile_spec] + [_const_spec(c.shape) for c in consts] + [rope_spec, rope_spec]
        + [_const_spec(c.shape) for c in (dmat, qdec, kdec, cdec)],
        out_specs=tile_spec,
        out_shape=jax.ShapeDtypeStruct(x.shape, x.dtype),
        scratch_shapes=[
            pltpu.VMEM((TILE + SUBLANES, N_PROJ), _F32),
            pltpu.VMEM((RET_HEADS, RET_DK, RET_DV), _F32),
            pltpu.VMEM((DN_HEADS, DN_DK, DN_DV), _F32),
        ],
        compiler_params=pltpu.CompilerParams(
            dimension_semantics=("arbitrary", "arbitrary"),
            vmem_limit_bytes=VMEM_LIMIT_BYTES),
        name="hybrid_layer",
    )(x, *consts, cos, sin, dmat, qdec, kdec, cdec)


def _tables(seq):
    half = RET_DK // 2
    inv_freq = jnp.power(ROPE_BASE, -jnp.arange(half, dtype=_F32) / half)
    ang = jnp.arange(seq).astype(_F32)[:, None] * inv_freq[None, :]
    cos = jnp.concatenate([jnp.cos(ang), jnp.cos(ang)], axis=1)
    sin = jnp.concatenate([-jnp.sin(ang), jnp.sin(ang)], axis=1)
    log_g = jnp.log1p(-jnp.power(2.0, -5.0 - jnp.arange(RET_HEADS, dtype=_F32)))
    idx = jnp.arange(TILE, dtype=_F32)
    rel = idx[:, None] - idx[None, :]
    dmat = jnp.where(rel[None] >= 0, jnp.exp(jnp.maximum(rel, 0.0)[None] * log_g[:, None, None]), 0.0)
    qdec = jnp.broadcast_to(jnp.exp((idx + 1.0)[None, :] * log_g[:, None])[:, :, None],
                            (RET_HEADS, TILE, RET_DK))
    kdec = jnp.broadcast_to(jnp.exp((TILE - 1 - idx)[None, :] * log_g[:, None])[:, :, None],
                            (RET_HEADS, TILE, RET_DK))
    cdec = jnp.broadcast_to(jnp.exp(TILE * log_g)[:, None, None], (RET_HEADS, 1, RET_DV))
    return cos, sin, dmat, qdec, kdec, cdec


def _lane_row(vals, first_lane):
    return jnp.zeros((1, LANES), _F32).at[0, first_lane:first_lane + vals.shape[0]].set(vals.astype(_F32))


def kernel(x, norm_gain, w_in, conv_w, a_log, dt_bias, dn_norm_gain, w_up_ret, w_up_dn, w_out, final_gain):
    assert x.shape[1] % TILE == 0 and x.shape[2] == D_MODEL
    n_main = C_GR
    n_bd = 2 * DN_HEADS
    tables = _tables(x.shape[1])
    for l in range(DEPTH):
        w = w_in[l]
        win = jnp.concatenate(
            [w[:, :n_main], w[:, n_main + n_bd:], w[:, n_main:n_main + n_bd],
             jnp.zeros((D_MODEL, LANES - n_bd), w.dtype)], axis=1).astype(_BF16)
        x = _layer(
            x, norm_gain[l][None, :], win, conv_w[l], _lane_row(a_log[l], A_LANE), _lane_row(dt_bias[l], A_LANE),
            dn_norm_gain[l][None, :], w_up_ret[l].astype(_BF16), w_up_dn[l].astype(_BF16),
            w_out[l].astype(_BF16), final_gain[None, :], tables, final_norm=(l == DEPTH - 1))
    return x
```

```python
import functools

import jax
import jax.numpy as jnp
from jax import lax
from jax.experimental import pallas as pl
from jax.experimental.pallas import tpu as pltpu

D_MODEL = 1024
DEPTH = 2
RET_HEADS = 4
RET_DK = 128
RET_DV = 256
DN_HEADS = 8
DN_DK = 128
DN_DV = 128
DN_CHUNK = 64
CONV_K = 4
ROPE_BASE = 10000.0
EPS = 1e-6

LANES = 128
SUBLANES = 8
TILE = 256
N_CHUNKS = TILE // DN_CHUNK

C_RQ = 0
C_RK = C_RQ + RET_HEADS * RET_DK
C_RV = C_RK + RET_HEADS * RET_DK
C_RZ = C_RV + RET_HEADS * RET_DV
C_DQ = C_RZ + RET_HEADS * RET_DV
C_DK = C_DQ + DN_HEADS * DN_DK
C_DV = C_DK + DN_HEADS * DN_DK
C_DZ = C_DV + DN_HEADS * DN_DV
C_GR = C_DZ + DN_HEADS * DN_DV
C_GD = C_GR + D_MODEL
C_BD = C_GD + D_MODEL
N_PROJ = C_BD + LANES
P_BD = C_DZ
P_WIDTH = P_BD + LANES
PROJ_CHUNK = 512
A_LANE = DN_HEADS

VMEM_LIMIT_BYTES = 56 * 1024 * 1024

_F32 = jnp.float32
_BF16 = jnp.bfloat16


def _dot(a, b):
    return jnp.dot(a.astype(_BF16), b.astype(_BF16), preferred_element_type=_F32)


def _dot_nt(a, b):
    return lax.dot_general(a.astype(_BF16), b.astype(_BF16), (((1,), (1,)), ((), ())),
                           preferred_element_type=_F32)


def _dot_tn(a, b):
    return lax.dot_general(a.astype(_BF16), b.astype(_BF16), (((0,), (0,)), ((), ())),
                           preferred_element_type=_F32)


def _silu(x):
    return x * jax.nn.sigmoid(x)


def _softplus(x):
    return jnp.maximum(x, 0.0) + jnp.log1p(jnp.exp(-jnp.abs(x)))


def _block_diag(wide, block_mask):
    return jnp.where(block_mask, jnp.concatenate([wide] * N_CHUNKS, axis=0), jnp.zeros((), wide.dtype))


def _fold(full):
    out = full[0:DN_CHUNK]
    for c in range(1, N_CHUNKS):
        out = out + full[c * DN_CHUNK:(c + 1) * DN_CHUNK]
    return out


def _layer_kernel(x_ref, ng_ref, win_ref, convw_ref, alog_ref, dtb_ref, dng_ref, wur_ref, wud_ref,
                  wo_ref, fg_ref, cos_ref, sin_ref, dmat_ref, qdec_ref, kdec_ref, cdec_ref,
                  out_ref, proj_ref, hb_ref, or_ref, od_ref, nmat_ref, tinv_ref, attn_ref, rhs_ref,
                  w_ref, u_ref, qd_ref, kd_ref, rstate_ref, dstate_ref, *, final_norm):
    rows = pl.ds(SUBLANES, TILE)

    def dz_view(hh):
        return proj_ref.at[rows, C_DQ + hh * DN_DV:C_DQ + (hh + 1) * DN_DV]

    @pl.when(pl.program_id(1) == 0)
    def _():
        proj_ref[0:SUBLANES, :] = jnp.zeros((SUBLANES, P_WIDTH), _F32)
        rstate_ref[...] = jnp.zeros_like(rstate_ref)
        dstate_ref[...] = jnp.zeros_like(dstate_ref)

    x = x_ref[...]
    h = x * lax.rsqrt(jnp.mean(x * x, axis=-1, keepdims=True) + EPS) * ng_ref[...]
    hb_ref[...] = h.astype(_BF16)

    def project(w_c0, width):
        return jnp.dot(hb_ref[...], win_ref[:, w_c0:w_c0 + width], preferred_element_type=_F32)

    for c0 in range(0, C_DZ, PROJ_CHUNK):
        proj_ref[rows, c0:c0 + PROJ_CHUNK] = project(c0, PROJ_CHUNK)
    proj_ref[rows, P_BD:P_BD + LANES] = project(C_BD, LANES)

    cos = cos_ref[...]
    sin = sin_ref[...]
    qs, ks, scores = [], [], []
    for hh in range(RET_HEADS):
        q = proj_ref[rows, C_RQ + hh * RET_DK:C_RQ + (hh + 1) * RET_DK]
        k = proj_ref[rows, C_RK + hh * RET_DK:C_RK + (hh + 1) * RET_DK]
        q = q * cos + pltpu.roll(q, RET_DK // 2, 1) * sin
        k = (k * cos + pltpu.roll(k, RET_DK // 2, 1) * sin) * (RET_DK ** -0.5)
        scores.append(_dot_nt(q, k) * dmat_ref[hh])
        qs.append(q)
        ks.append(k)
    outs = []
    for hh in range(RET_HEADS):
        v = proj_ref[rows, C_RV + hh * RET_DV:C_RV + (hh + 1) * RET_DV]
        outs.append(_dot(jnp.concatenate([scores[hh], qs[hh] * qdec_ref[hh]], axis=1),
                         jnp.concatenate([v, rstate_ref[hh]], axis=0)))
    for hh in range(RET_HEADS):
        v = proj_ref[rows, C_RV + hh * RET_DV:C_RV + (hh + 1) * RET_DV]
        rstate_ref[hh] = rstate_ref[hh] * cdec_ref[hh] + _dot_tn(ks[hh] * kdec_ref[hh], v)
    for hh in range(RET_HEADS):
        o = outs[hh]
        z = proj_ref[rows, C_RZ + hh * RET_DV:C_RZ + (hh + 1) * RET_DV]
        mu = jnp.mean(o, axis=-1, keepdims=True)
        d = o - mu
        var = jnp.mean(d * d, axis=-1, keepdims=True)
        or_ref[:, hh * RET_DV:(hh + 1) * RET_DV] = (d * lax.rsqrt(var + EPS) * _silu(z)).astype(_BF16)

    bd = proj_ref[rows, P_BD:P_BD + LANES]
    beta_all = jax.nn.sigmoid(bd)
    g_all = -jnp.exp(alog_ref[...]) * _softplus(bd + dtb_ref[...])
    row_in_chunk = lax.broadcasted_iota(jnp.int32, (TILE, LANES), 0) & (DN_CHUNK - 1)
    gc_all = g_all
    shift = 1
    while shift < DN_CHUNK:
        gc_all = gc_all + jnp.where(row_in_chunk >= shift, pltpu.roll(gc_all, shift, 0), 0.0)
        shift *= 2
    gl_all = jnp.broadcast_to(
        gc_all.reshape(N_CHUNKS, DN_CHUNK, LANES)[:, DN_CHUNK - 1:DN_CHUNK, :],
        (N_CHUNKS, DN_CHUNK, LANES)).reshape(TILE, LANES)
    gc_t = gc_all.T
    e_gl_all = jnp.exp(gl_all)

    ri = lax.broadcasted_iota(jnp.int32, (TILE, TILE), 0)
    ci = lax.broadcasted_iota(jnp.int32, (TILE, TILE), 1)
    same_chunk = (ri // DN_CHUNK) == (ci // DN_CHUNK)
    causal = same_chunk & (ri >= ci)
    strict = same_chunk & (ri > ci)
    wr = lax.broadcasted_iota(jnp.int32, (DN_CHUNK, TILE), 0)
    wc = lax.broadcasted_iota(jnp.int32, (DN_CHUNK, TILE), 1) & (DN_CHUNK - 1)
    eye_wide = (wr == wc).astype(_F32)
    cw = convw_ref[...]

    def conv_silu(c0):
        y = (cw[3:4, c0:c0 + LANES] * proj_ref[pl.ds(SUBLANES, TILE), C_DQ + c0:C_DQ + c0 + LANES]
             + cw[2:3, c0:c0 + LANES] * proj_ref[pl.ds(SUBLANES - 1, TILE), C_DQ + c0:C_DQ + c0 + LANES]
             + cw[1:2, c0:c0 + LANES] * proj_ref[pl.ds(SUBLANES - 2, TILE), C_DQ + c0:C_DQ + c0 + LANES]
             + cw[0:1, c0:c0 + LANES] * proj_ref[pl.ds(SUBLANES - 3, TILE), C_DQ + c0:C_DQ + c0 + LANES])
        return _silu(y)

    for hh in range(DN_HEADS):
        q = conv_silu(hh * DN_DK)
        k = conv_silu(C_DK - C_DQ + hh * DN_DK)
        v = conv_silu(C_DV - C_DQ + hh * DN_DV)
        q = q * lax.rsqrt(jnp.sum(q * q, axis=-1, keepdims=True) + EPS) * (DN_DK ** -0.5)
        k = k * lax.rsqrt(jnp.sum(k * k, axis=-1, keepdims=True) + EPS)
        beta = beta_all[:, hh:hh + 1]
        gc = gc_all[:, A_LANE + hh:A_LANE + hh + 1]
        gl = gl_all[:, A_LANE + hh:A_LANE + hh + 1]
        gc_row = gc_t[A_LANE + hh:A_LANE + hh + 1, :]
        decay = jnp.where(causal, jnp.exp(jnp.where(causal, gc - gc_row, 0.0)), 0.0)
        k_beta = k * beta
        gram = _dot_nt(jnp.concatenate([k_beta, q], axis=0), k)
        lmat = _fold(jnp.where(strict, gram[:TILE] * decay, 0.0))
        attn_ref[hh] = _fold(gram[TILE:] * decay).astype(_BF16)
        nmat_ref[hh] = (-lmat).astype(_BF16)
        tinv_ref[hh] = eye_wide - lmat
        e_gc = jnp.exp(gc)
        rhs_ref[hh] = jnp.concatenate([v * beta, k_beta * e_gc], axis=1).astype(_BF16)
        qd_ref[hh] = (q * e_gc).astype(_BF16)
        kd_ref[hh] = (k * jnp.exp(gl - gc)).astype(_BF16)

    proj_ref[0:SUBLANES, C_DQ:C_DZ] = proj_ref[TILE:TILE + SUBLANES, C_DQ:C_DZ]

    for c0 in range(0, DN_HEADS * DN_DV, PROJ_CHUNK):
        proj_ref[rows, C_DQ + c0:C_DQ + c0 + PROJ_CHUNK] = project(C_DZ + c0, PROJ_CHUNK)

    for hh in range(DN_HEADS):
        npow = nmat_ref[hh]
        nmat_ref[hh] = jnp.dot(npow, _block_diag(npow, same_chunk), preferred_element_type=_F32).astype(_BF16)
    step = 2
    while step < DN_CHUNK:
        last = step * 2 >= DN_CHUNK
        for hh in range(DN_HEADS):
            npow = nmat_ref[hh]
            tinv = tinv_ref[hh]
            lhs = tinv.astype(_BF16) if last else jnp.concatenate([tinv.astype(_BF16), npow], axis=0)
            prod = jnp.dot(lhs, _block_diag(npow, same_chunk), preferred_element_type=_F32)
            tinv_ref[hh] = tinv + prod[:DN_CHUNK]
            if not last:
                nmat_ref[hh] = prod[DN_CHUNK:].astype(_BF16)
        step *= 2

    for hh in range(DN_HEADS):
        sol = jnp.dot(_block_diag(tinv_ref[hh].astype(_BF16), same_chunk), rhs_ref[hh],
                      preferred_element_type=_F32)
        u_ref[hh] = sol[:, :DN_DV]
        w_ref[hh] = sol[:, DN_DV:].astype(_BF16)

    for c in range(N_CHUNKS):
        r0 = c * DN_CHUNK
        l0 = (r0 // LANES) * LANES
        wss = []
        for hh in range(DN_HEADS):
            wq = jnp.concatenate([w_ref[hh, r0:r0 + DN_CHUNK, :], qd_ref[hh, r0:r0 + DN_CHUNK, :]], axis=0)
            wss.append(jnp.dot(wq, dstate_ref[hh].astype(_BF16), preferred_element_type=_F32))
        v_new = [(u_ref[hh, r0:r0 + DN_CHUNK, :] - wss[hh][:DN_CHUNK]).astype(_BF16) for hh in range(DN_HEADS)]
        for hh in range(DN_HEADS):
            attn_cc = attn_ref[hh, :, l0:l0 + LANES][:, r0 - l0:r0 - l0 + DN_CHUNK]
            u_ref[hh, r0:r0 + DN_CHUNK, :] = wss[hh][DN_CHUNK:] + jnp.dot(
                attn_cc, v_new[hh], preferred_element_type=_F32)
        for hh in range(DN_HEADS):
            e_gl = e_gl_all[r0:r0 + 1, A_LANE + hh:A_LANE + hh + 1]
            dstate_ref[hh] = dstate_ref[hh] * e_gl + lax.dot_general(
                kd_ref[hh, r0:r0 + DN_CHUNK, :], v_new[hh], (((0,), (0,)), ((), ())),
                preferred_element_type=_F32)

    for hh in range(DN_HEADS):
        o = u_ref[hh]
        o = o * lax.rsqrt(jnp.mean(o * o, axis=-1, keepdims=True) + EPS) * dng_ref[...]
        od_ref[:, hh * DN_DV:(hh + 1) * DN_DV] = (o * _silu(dz_view(hh)[...])).astype(_BF16)

    ys = []
    for c0 in range(0, D_MODEL, PROJ_CHUNK):
        gate_r = jax.nn.sigmoid(project(C_GR + c0, PROJ_CHUNK))
        gate_d = jax.nn.sigmoid(project(C_GD + c0, PROJ_CHUNK))
        ys.append((gate_r * jnp.dot(or_ref[...], wur_ref[:, c0:c0 + PROJ_CHUNK], preferred_element_type=_F32)
                   + gate_d * jnp.dot(od_ref[...], wud_ref[:, c0:c0 + PROJ_CHUNK], preferred_element_type=_F32)
                   ).astype(_BF16))
    y = jnp.concatenate(ys, axis=1)
    out = x_ref[...] + jnp.dot(y, wo_ref[...], preferred_element_type=_F32)
    if final_norm:
        out = out * lax.rsqrt(jnp.mean(out * out, axis=-1, keepdims=True) + EPS) * fg_ref[...]
    out_ref[...] = out


def _const_spec(shape):
    zeros = (0,) * len(shape)
    return pl.BlockSpec(shape, lambda b, t: zeros, pipeline_mode=pl.Buffered(1))


def _layer(x, ng, win, convw, alog, dtb, dng, wur, wud, wo, fg, tables, *, final_norm):
    batch, seq, _ = x.shape
    cos, sin, dmat, qdec, kdec, cdec = tables
    consts = (ng, win, convw, alog, dtb, dng, wur, wud, wo, fg)
    tile_spec = pl.BlockSpec((None, TILE, D_MODEL), lambda b, t: (b, t, 0))
    rope_spec = pl.BlockSpec((TILE, LANES), lambda b, t: (t, 0))
    return pl.pallas_call(
        functools.partial(_layer_kernel, final_norm=final_norm),
        grid=(batch, seq // TILE),
        in_specs=[tile_spec] + [_const_spec(c.shape) for c in consts] + [rope_spec, rope_spec]
        + [_const_spec(c.shape) for c in (dmat, qdec, kdec, cdec)],
        out_specs=tile_spec,
        out_shape=jax.ShapeDtypeStruct(x.shape, x.dtype),
        scratch_shapes=[
            pltpu.VMEM((TILE + SUBLANES, P_WIDTH), _F32),
            pltpu.VMEM((TILE, D_MODEL), _BF16),
            pltpu.VMEM((TILE, RET_HEADS * RET_DV), _BF16),
            pltpu.VMEM((TILE, DN_HEADS * DN_DV), _BF16),
            pltpu.VMEM((DN_HEADS, DN_CHUNK, TILE), _BF16),
            pltpu.VMEM((DN_HEADS, DN_CHUNK, TILE), _F32),
            pltpu.VMEM((DN_HEADS, DN_CHUNK, TILE), _BF16),
            pltpu.VMEM((DN_HEADS, TILE, DN_DV + DN_DK), _BF16),
            pltpu.VMEM((DN_HEADS, TILE, DN_DK), _BF16),
            pltpu.VMEM((DN_HEADS, TILE, DN_DV), _F32),
            pltpu.VMEM((DN_HEADS, TILE, DN_DK), _BF16),
            pltpu.VMEM((DN_HEADS, TILE, DN_DK), _BF16),
            pltpu.VMEM((RET_HEADS, RET_DK, RET_DV), _F32),
            pltpu.VMEM((DN_HEADS, DN_DK, DN_DV), _F32),
        ],
        compiler_params=pltpu.CompilerParams(
            dimension_semantics=("arbitrary", "arbitrary"),
            vmem_limit_bytes=VMEM_LIMIT_BYTES),
        name="hybrid_layer",
    )(x, *consts, cos, sin, dmat, qdec, kdec, cdec)


def _tables(seq):
    half = RET_DK // 2
    inv_freq = jnp.power(ROPE_BASE, -jnp.arange(half, dtype=_F32) / half)
    ang = jnp.arange(seq).astype(_F32)[:, None] * inv_freq[None, :]
    cos = jnp.concatenate([jnp.cos(ang), jnp.cos(ang)], axis=1)
    sin = jnp.concatenate([-jnp.sin(ang), jnp.sin(ang)], axis=1)
    log_g = jnp.log1p(-jnp.power(2.0, -5.0 - jnp.arange(RET_HEADS, dtype=_F32)))
    idx = jnp.arange(TILE, dtype=_F32)
    rel = idx[:, None] - idx[None, :]
    dmat = jnp.where(rel[None] >= 0, jnp.exp(jnp.maximum(rel, 0.0)[None] * log_g[:, None, None]), 0.0)
    qdec = jnp.broadcast_to(jnp.exp((idx + 1.0)[None, :] * log_g[:, None])[:, :, None],
                            (RET_HEADS, TILE, RET_DK))
    kdec = jnp.broadcast_to(jnp.exp((TILE - 1 - idx)[None, :] * log_g[:, None])[:, :, None],
                            (RET_HEADS, TILE, RET_DK))
    cdec = jnp.broadcast_to(jnp.exp(TILE * log_g)[:, None, None], (RET_HEADS, 1, RET_DV))
    return cos, sin, dmat, qdec, kdec, cdec


def _lane_row(vals, first_lane):
    return jnp.zeros((1, LANES), _F32).at[0, first_lane:first_lane + vals.shape[0]].set(vals.astype(_F32))


def kernel(x, norm_gain, w_in, conv_w, a_log, dt_bias, dn_norm_gain, w_up_ret, w_up_dn, w_out, final_gain):
    assert x.shape[1] % TILE == 0 and x.shape[2] == D_MODEL
    n_main = C_GR
    n_bd = 2 * DN_HEADS
    tables = _tables(x.shape[1])
    for l in range(DEPTH):
        w = w_in[l]
        win = jnp.concatenate(
            [w[:, :n_main], w[:, n_main + n_bd:], w[:, n_main:n_main + n_bd],
             jnp.zeros((D_MODEL, LANES - n_bd), w.dtype)], axis=1).astype(_BF16)
        x = _layer(
            x, norm_gain[l][None, :], win, conv_w[l], _lane_row(a_log[l], A_LANE), _lane_row(dt_bias[l], A_LANE),
            dn_norm_gain[l][None, :], w_up_ret[l].astype(_BF16), w_up_dn[l].astype(_BF16),
            w_out[l].astype(_BF16), final_gain[None, :], tables, final_norm=(l == DEPTH - 1))
    return x
```

```python
import functools

import numpy as np

import jax
import jax.numpy as jnp
from jax import lax
from jax.experimental import pallas as pl
from jax.experimental.pallas import tpu as pltpu

D_MODEL = 1024
DEPTH = 2
RET_HEADS = 4
RET_DK = 128
RET_DV = 256
DN_HEADS = 8
DN_DK = 128
DN_DV = 128
DN_CHUNK = 64
CONV_K = 4
ROPE_BASE = 10000.0
EPS = 1e-6

LANES = 128
SUBLANES = 8
TILE = 256
N_CHUNKS = TILE // DN_CHUNK

C_RQ = 0
C_RK = C_RQ + RET_HEADS * RET_DK
C_RV = C_RK + RET_HEADS * RET_DK
C_RZ = C_RV + RET_HEADS * RET_DV
C_DQ = C_RZ + RET_HEADS * RET_DV
C_DK = C_DQ + DN_HEADS * DN_DK
C_DV = C_DK + DN_HEADS * DN_DK
C_DZ = C_DV + DN_HEADS * DN_DV
C_G = C_DZ + DN_HEADS * DN_DV
G_GR = 2 * DN_HEADS
G_GD = G_GR + D_MODEL
IN_W = C_G + G_GD + D_MODEL
PROJ_CHUNK = 512
N_PROJ = -(-IN_W // LANES) * LANES
G_WIDTH = N_PROJ - C_G
A_LANE = DN_HEADS
P_DZ = C_DQ
P_WIDTH = P_DZ + DN_HEADS * DN_DV
CONV_BLOCKS = (C_DZ - C_DQ) // LANES

VMEM_LIMIT_BYTES = 56 * 1024 * 1024

_F32 = jnp.float32
_BF16 = jnp.bfloat16


def _dot(a, b):
    return jnp.dot(a.astype(_BF16), b.astype(_BF16), preferred_element_type=_F32)


def _dot_nt(a, b):
    return lax.dot_general(a.astype(_BF16), b.astype(_BF16), (((1,), (1,)), ((), ())),
                           preferred_element_type=_F32)


def _dot_tn(a, b):
    return lax.dot_general(a.astype(_BF16), b.astype(_BF16), (((0,), (0,)), ((), ())),
                           preferred_element_type=_F32)


def _sigmoid(x):
    return 0.5 + 0.5 * jnp.tanh(0.5 * x)


def _silu(x):
    hx = 0.5 * x
    return hx + hx * jnp.tanh(hx)


def _softplus(x):
    return jnp.maximum(x, 0.0) + jnp.log1p(jnp.exp(-jnp.abs(x)))


def _block_diag(wide, block_mask):
    return jnp.where(block_mask, jnp.concatenate([wide] * N_CHUNKS, axis=0), jnp.zeros((), wide.dtype))


def _fold(full):
    out = full[0:DN_CHUNK]
    for c in range(1, N_CHUNKS):
        out = out + full[c * DN_CHUNK:(c + 1) * DN_CHUNK]
    return out


def _layer_kernel(x_ref, ng_ref, win_ref, convw_ref, alog_ref, dtb_ref, dng_ref, wur_ref, wud_ref,
                  wo_ref, fg_ref, cos_ref, sin_ref, dmat_ref, qdec_ref, kdec_ref, cdec_ref,
                  out_ref, proj_ref, cin_ref, g_ref, hb_ref, or_ref, od_ref, nmat_ref, tinv_ref,
                  attn_ref, rhs_ref, w_ref, u_ref, qd_ref, kd_ref, rstate_ref, dstate_ref, *, final_norm):
    hist = pl.ds(SUBLANES, TILE)

    @pl.when(pl.program_id(1) == 0)
    def _():
        cin_ref[:, 0:SUBLANES, :] = jnp.zeros((CONV_BLOCKS, SUBLANES, LANES), _F32)
        rstate_ref[...] = jnp.zeros_like(rstate_ref)
        dstate_ref[...] = jnp.zeros_like(dstate_ref)

    x = x_ref[...]
    h = x * lax.rsqrt(jnp.mean(x * x, axis=-1, keepdims=True) + EPS) * ng_ref[...]
    hb_ref[...] = h.astype(_BF16)

    def project(w_c0, width=PROJ_CHUNK):
        return jnp.dot(hb_ref[...], win_ref[:, w_c0:w_c0 + width], preferred_element_type=_F32)

    def project_ret(c0):
        proj_ref[:, c0:c0 + PROJ_CHUNK] = project(c0)

    def project_conv(blk0):
        res = project(C_DQ + blk0 * LANES)
        for j in range(PROJ_CHUNK // LANES):
            cin_ref[blk0 + j, hist, :] = res[:, j * LANES:(j + 1) * LANES]

    def project_dz(c0):
        proj_ref[:, P_DZ + c0:P_DZ + c0 + PROJ_CHUNK] = project(C_DZ + c0)

    def project_g(c0, width=PROJ_CHUNK):
        g_ref[:, c0:c0 + width] = project(C_G + c0, width)

    project_ret(C_RQ)
    project_ret(C_RK)
    cos = cos_ref[...]
    sin = sin_ref[...]
    qs, ks, scores = [], [], []
    for hh in range(RET_HEADS):
        q = proj_ref[:, C_RQ + hh * RET_DK:C_RQ + (hh + 1) * RET_DK]
        k = proj_ref[:, C_RK + hh * RET_DK:C_RK + (hh + 1) * RET_DK]
        q = q * cos + pltpu.roll(q, RET_DK // 2, 1) * sin
        k = (k * cos + pltpu.roll(k, RET_DK // 2, 1) * sin) * (RET_DK ** -0.5)
        scores.append(_dot_nt(q, k) * dmat_ref[hh])
        qs.append(q)
        ks.append(k)
    project_ret(C_RV)
    project_ret(C_RV + PROJ_CHUNK)
    outs = []
    for hh in range(RET_HEADS):
        v = proj_ref[:, C_RV + hh * RET_DV:C_RV + (hh + 1) * RET_DV]
        outs.append(_dot(jnp.concatenate([scores[hh], qs[hh] * qdec_ref[hh]], axis=1),
                         jnp.concatenate([v, rstate_ref[hh]], axis=0)))
    project_ret(C_RZ)
    project_ret(C_RZ + PROJ_CHUNK)
    for hh in range(RET_HEADS):
        v = proj_ref[:, C_RV + hh * RET_DV:C_RV + (hh + 1) * RET_DV]
        rstate_ref[hh] = rstate_ref[hh] * cdec_ref[hh] + _dot_tn(ks[hh] * kdec_ref[hh], v)
    project_g(0)
    for blk0 in (0, DN_HEADS, 2 * DN_HEADS):
        project_conv(blk0)
    for hh in range(RET_HEADS):
        o = outs[hh]
        z = proj_ref[:, C_RZ + hh * RET_DV:C_RZ + (hh + 1) * RET_DV]
        mu = jnp.mean(o, axis=-1, keepdims=True)
        d = o - mu
        var = jnp.mean(d * d, axis=-1, keepdims=True)
        or_ref[:, hh * RET_DV:(hh + 1) * RET_DV] = (d * lax.rsqrt(var + EPS) * _silu(z)).astype(_BF16)

    bd = g_ref[:, 0:LANES]
    beta_all = _sigmoid(bd)
    g_all = -jnp.exp(alog_ref[...]) * _softplus(bd + dtb_ref[...])
    row_in_chunk = lax.broadcasted_iota(jnp.int32, (TILE, LANES), 0) & (DN_CHUNK - 1)
    gc_all = g_all
    shift = 1
    while shift < DN_CHUNK:
        gc_all = gc_all + jnp.where(row_in_chunk >= shift, pltpu.roll(gc_all, shift, 0), 0.0)
        shift *= 2
    gl_all = jnp.broadcast_to(
        gc_all.reshape(N_CHUNKS, DN_CHUNK, LANES)[:, DN_CHUNK - 1:DN_CHUNK, :],
        (N_CHUNKS, DN_CHUNK, LANES)).reshape(TILE, LANES)
    gc_t = gc_all.T
    e_gl_all = jnp.exp(gl_all)

    ri = lax.broadcasted_iota(jnp.int32, (TILE, TILE), 0)
    ci = lax.broadcasted_iota(jnp.int32, (TILE, TILE), 1)
    same_chunk = (ri // DN_CHUNK) == (ci // DN_CHUNK)
    causal = same_chunk & (ri >= ci)
    strict = same_chunk & (ri > ci)
    wr = lax.broadcasted_iota(jnp.int32, (DN_CHUNK, TILE), 0)
    wc = lax.broadcasted_iota(jnp.int32, (DN_CHUNK, TILE), 1) & (DN_CHUNK - 1)
    eye_wide = (wr == wc).astype(_F32)
    cw = convw_ref[...]

    def conv_silu(blk):
        c0 = blk * LANES
        y = (cw[3:4, c0:c0 + LANES] * cin_ref[blk, pl.ds(SUBLANES, TILE), :]
             + cw[2:3, c0:c0 + LANES] * cin_ref[blk, pl.ds(SUBLANES - 1, TILE), :]
             + cw[1:2, c0:c0 + LANES] * cin_ref[blk, pl.ds(SUBLANES - 2, TILE), :]
             + cw[0:1, c0:c0 + LANES] * cin_ref[blk, pl.ds(SUBLANES - 3, TILE), :])
        return _silu(y)

    filler = [
        lambda: project_conv(RET_HEADS),
        lambda: project_conv(DN_HEADS + RET_HEADS),
        lambda: project_conv(2 * DN_HEADS + RET_HEADS),
        lambda: project_dz(0),
        lambda: project_dz(PROJ_CHUNK),
        lambda: project_g(PROJ_CHUNK),
        lambda: project_g(2 * PROJ_CHUNK),
        lambda: (project_g(3 * PROJ_CHUNK), project_g(4 * PROJ_CHUNK, G_WIDTH - 4 * PROJ_CHUNK)),
    ]

    for hh in range(DN_HEADS):
        q = conv_silu(hh)
        k = conv_silu(DN_HEADS + hh)
        v = conv_silu(2 * DN_HEADS + hh)
        q = q * lax.rsqrt(jnp.sum(q * q, axis=-1, keepdims=True) + EPS) * (DN_DK ** -0.5)
        k = k * lax.rsqrt(jnp.sum(k * k, axis=-1, keepdims=True) + EPS)
        beta = beta_all[:, hh:hh + 1]
        gc = gc_all[:, A_LANE + hh:A_LANE + hh + 1]
        gl = gl_all[:, A_LANE + hh:A_LANE + hh + 1]
        gc_row = gc_t[A_LANE + hh:A_LANE + hh + 1, :]
        decay = jnp.where(causal, jnp.exp(jnp.where(causal, gc - gc_row, 0.0)), 0.0)
        k_beta = k * beta
        gram = _dot_nt(jnp.concatenate([k_beta, q], axis=0), k)
        lmat = _fold(jnp.where(strict, gram[:TILE] * decay, 0.0))
        attn_ref[hh] = _fold(gram[TILE:] * decay).astype(_BF16)
        nmat_ref[hh] = (-lmat).astype(_BF16)
        tinv_ref[hh] = eye_wide - lmat
        e_gc = jnp.exp(gc)
        rhs_ref[hh] = jnp.concatenate([v * beta, k_beta * e_gc], axis=1).astype(_BF16)
        qd_ref[hh] = (q * e_gc).astype(_BF16)
        kd_ref[hh] = (k * jnp.exp(gl - gc)).astype(_BF16)
        filler[hh]()

    cin_ref[:, 0:SUBLANES, :] = cin_ref[:, TILE:TILE + SUBLANES, :]

    for hh in range(DN_HEADS):
        npow = nmat_ref[hh]
        nmat_ref[hh] = jnp.dot(npow, _block_diag(npow, same_chunk), preferred_element_type=_F32).astype(_BF16)
    step = 2
    while step < DN_CHUNK:
        last = step * 2 >= DN_CHUNK
        for hh in range(DN_HEADS):
            npow = nmat_ref[hh]
            tinv = tinv_ref[hh]
            lhs = tinv.astype(_BF16) if last else jnp.concatenate([tinv.astype(_BF16), npow], axis=0)
            prod = jnp.dot(lhs, _block_diag(npow, same_chunk), preferred_element_type=_F32)
            tinv_ref[hh] = tinv + prod[:DN_CHUNK]
            if not last:
                nmat_ref[hh] = prod[DN_CHUNK:].astype(_BF16)
        step *= 2

    for hh in range(DN_HEADS):
        sol = jnp.dot(_block_diag(tinv_ref[hh].astype(_BF16), same_chunk), rhs_ref[hh],
                      preferred_element_type=_F32)
        u_ref[hh] = sol[:, :DN_DV]
        w_ref[hh] = sol[:, DN_DV:].astype(_BF16)

    for c in range(N_CHUNKS):
        r0 = c * DN_CHUNK
        l0 = (r0 // LANES) * LANES
        wss = []
        for hh in range(DN_HEADS):
            wq = jnp.concatenate([w_ref[hh, r0:r0 + DN_CHUNK, :], qd_ref[hh, r0:r0 + DN_CHUNK, :]], axis=0)
            wss.append(jnp.dot(wq, dstate_ref[hh].astype(_BF16), preferred_element_type=_F32))
        v_new = [(u_ref[hh, r0:r0 + DN_CHUNK, :] - wss[hh][:DN_CHUNK]).astype(_BF16) for hh in range(DN_HEADS)]
        for hh in range(DN_HEADS):
            attn_cc = attn_ref[hh, :, l0:l0 + LANES][:, r0 - l0:r0 - l0 + DN_CHUNK]
            u_ref[hh, r0:r0 + DN_CHUNK, :] = wss[hh][DN_CHUNK:] + jnp.dot(
                attn_cc, v_new[hh], preferred_element_type=_F32)
        for hh in range(DN_HEADS):
            e_gl = e_gl_all[r0:r0 + 1, A_LANE + hh:A_LANE + hh + 1]
            dstate_ref[hh] = dstate_ref[hh] * e_gl + lax.dot_general(
                kd_ref[hh, r0:r0 + DN_CHUNK, :], v_new[hh], (((0,), (0,)), ((), ())),
                preferred_element_type=_F32)

    for hh in range(DN_HEADS):
        o = u_ref[hh]
        o = o * lax.rsqrt(jnp.mean(o * o, axis=-1, keepdims=True) + EPS) * dng_ref[...]
        z = proj_ref[:, P_DZ + hh * DN_DV:P_DZ + (hh + 1) * DN_DV]
        od_ref[:, hh * DN_DV:(hh + 1) * DN_DV] = (o * _silu(z)).astype(_BF16)

    ys = []
    for c0 in range(0, D_MODEL, PROJ_CHUNK):
        gate_r = _sigmoid(g_ref[:, G_GR + c0:G_GR + c0 + PROJ_CHUNK])
        gate_d = _sigmoid(g_ref[:, G_GD + c0:G_GD + c0 + PROJ_CHUNK])
        ys.append((gate_r * jnp.dot(or_ref[...], wur_ref[:, c0:c0 + PROJ_CHUNK], preferred_element_type=_F32)
                   + gate_d * jnp.dot(od_ref[...], wud_ref[:, c0:c0 + PROJ_CHUNK], preferred_element_type=_F32)
                   ).astype(_BF16))
    y = jnp.concatenate(ys, axis=1)
    out = x_ref[...] + jnp.dot(y, wo_ref[...], preferred_element_type=_F32)
    if final_norm:
        out = out * lax.rsqrt(jnp.mean(out * out, axis=-1, keepdims=True) + EPS) * fg_ref[...]
    out_ref[...] = out


def _const_spec(shape):
    zeros = (0,) * len(shape)
    return pl.BlockSpec(shape, lambda b, t: zeros, pipeline_mode=pl.Buffered(1))


def _layer(x, ng, win, convw, alog, dtb, dng, wur, wud, wo, fg, tables, *, final_norm):
    batch, seq, _ = x.shape
    cos, sin, dmat, qdec, kdec, cdec = tables
    consts = (ng, win, convw, alog, dtb, dng, wur, wud, wo, fg)
    tile_spec = pl.BlockSpec((None, TILE, D_MODEL), lambda b, t: (b, t, 0))
    rope_spec = pl.BlockSpec((TILE, LANES), lambda b, t: (t, 0))
    return pl.pallas_call(
        functools.partial(_layer_kernel, final_norm=final_norm),
        grid=(batch, seq // TILE),
        in_specs=[tile_spec] + [_const_spec(c.shape) for c in consts] + [rope_spec, rope_spec]
        + [_const_spec(c.shape) for c in (dmat, qdec, kdec, cdec)],
        out_specs=tile_spec,
        out_shape=jax.ShapeDtypeStruct(x.shape, x.dtype),
        scratch_shapes=[
            pltpu.VMEM((TILE, P_WIDTH), _F32),
            pltpu.VMEM((CONV_BLOCKS, TILE + SUBLANES, LANES), _F32),
            pltpu.VMEM((TILE, G_WIDTH), _F32),
            pltpu.VMEM((TILE, D_MODEL), _BF16),
            pltpu.VMEM((TILE, RET_HEADS * RET_DV), _BF16),
            pltpu.VMEM((TILE, DN_HEADS * DN_DV), _BF16),
            pltpu.VMEM((DN_HEADS, DN_CHUNK, TILE), _BF16),
            pltpu.VMEM((DN_HEADS, DN_CHUNK, TILE), _F32),
            pltpu.VMEM((DN_HEADS, DN_CHUNK, TILE), _BF16),
            pltpu.VMEM((DN_HEADS, TILE, DN_DV + DN_DK), _BF16),
            pltpu.VMEM((DN_HEADS, TILE, DN_DK), _BF16),
            pltpu.VMEM((DN_HEADS, TILE, DN_DV), _F32),
            pltpu.VMEM((DN_HEADS, TILE, DN_DK), _BF16),
            pltpu.VMEM((DN_HEADS, TILE, DN_DK), _BF16),
            pltpu.VMEM((RET_HEADS, RET_DK, RET_DV), _F32),
            pltpu.VMEM((DN_HEADS, DN_DK, DN_DV), _F32),
        ],
        compiler_params=pltpu.CompilerParams(
            dimension_semantics=("arbitrary", "arbitrary"),
            vmem_limit_bytes=VMEM_LIMIT_BYTES),
        name="hybrid_layer",
    )(x, *consts, cos, sin, dmat, qdec, kdec, cdec)


def _tables(seq):
    half = RET_DK // 2
    inv_freq = np.power(ROPE_BASE, -np.arange(half, dtype=np.float64) / half)
    ang = np.arange(seq, dtype=np.float64)[:, None] * inv_freq[None, :]
    cos = np.concatenate([np.cos(ang), np.cos(ang)], axis=1)
    sin = np.concatenate([-np.sin(ang), np.sin(ang)], axis=1)
    log_g = np.log1p(-np.power(2.0, -5.0 - np.arange(RET_HEADS, dtype=np.float64)))
    idx = np.arange(TILE, dtype=np.float64)
    rel = idx[:, None] - idx[None, :]
    dmat = np.where(rel[None] >= 0, np.exp(np.maximum(rel, 0.0)[None] * log_g[:, None, None]), 0.0)
    qdec = np.broadcast_to(np.exp((idx + 1.0)[None, :] * log_g[:, None])[:, :, None],
                           (RET_HEADS, TILE, RET_DK))
    kdec = np.broadcast_to(np.exp((TILE - 1 - idx)[None, :] * log_g[:, None])[:, :, None],
                           (RET_HEADS, TILE, RET_DK))
    cdec = np.broadcast_to(np.exp(TILE * log_g)[:, None, None], (RET_HEADS, 1, RET_DV))
    return tuple(jnp.asarray(np.ascontiguousarray(t), dtype=_F32) for t in (cos, sin, dmat, qdec, kdec, cdec))


def _lane_row(vals, first_lane):
    return jnp.zeros((1, LANES), _F32).at[0, first_lane:first_lane + vals.shape[0]].set(vals.astype(_F32))


def kernel(x, norm_gain, w_in, conv_w, a_log, dt_bias, dn_norm_gain, w_up_ret, w_up_dn, w_out, final_gain):
    assert x.shape[1] % TILE == 0 and x.shape[2] == D_MODEL and w_in.shape[2] == IN_W
    tables = _tables(x.shape[1])
    for l in range(DEPTH):
        win = jnp.pad(w_in[l].astype(_BF16), ((0, 0), (0, N_PROJ - IN_W)))
        x = _layer(
            x, norm_gain[l][None, :], win, conv_w[l], _lane_row(a_log[l], A_LANE), _lane_row(dt_bias[l], A_LANE),
            dn_norm_gain[l][None, :], w_up_ret[l].astype(_BF16), w_up_dn[l].astype(_BF16),
            w_out[l].astype(_BF16), final_gain[None, :], tables, final_norm=(l == DEPTH - 1))
    return x
```

```python
import functools

import numpy as np

import jax
import jax.numpy as jnp
from jax import lax
from jax.experimental import pallas as pl
from jax.experimental.pallas import tpu as pltpu

D_MODEL = 1024
DEPTH = 2
RET_HEADS = 4
RET_DK = 128
RET_DV = 256
DN_HEADS = 8
DN_DK = 128
DN_DV = 128
DN_CHUNK = 64
CONV_K = 4
ROPE_BASE = 10000.0
EPS = 1e-6

LANES = 128
SUBLANES = 8
TILE = 256
N_CHUNKS = TILE // DN_CHUNK

C_RQ = 0
C_RK = C_RQ + RET_HEADS * RET_DK
C_RV = C_RK + RET_HEADS * RET_DK
C_RZ = C_RV + RET_HEADS * RET_DV
C_DQ = C_RZ + RET_HEADS * RET_DV
C_DK = C_DQ + DN_HEADS * DN_DK
C_DV = C_DK + DN_HEADS * DN_DK
C_DZ = C_DV + DN_HEADS * DN_DV
C_G = C_DZ + DN_HEADS * DN_DV
G_GR = 2 * DN_HEADS
G_GD = G_GR + D_MODEL
IN_W = C_G + G_GD + D_MODEL
PROJ_CHUNK = 512
N_PROJ = -(-IN_W // LANES) * LANES
G_WIDTH = N_PROJ - C_G
A_LANE = DN_HEADS
P_DZ = C_DQ
P_WIDTH = P_DZ + DN_HEADS * DN_DV
CONV_BLOCKS = (C_DZ - C_DQ) // LANES

VMEM_LIMIT_BYTES = 56 * 1024 * 1024

_F32 = jnp.float32
_BF16 = jnp.bfloat16


def _dot(a, b):
    return jnp.dot(a.astype(_BF16), b.astype(_BF16), preferred_element_type=_F32)


def _dot_nt(a, b):
    return lax.dot_general(a.astype(_BF16), b.astype(_BF16), (((1,), (1,)), ((), ())),
                           preferred_element_type=_F32)


def _dot_tn(a, b):
    return lax.dot_general(a.astype(_BF16), b.astype(_BF16), (((0,), (0,)), ((), ())),
                           preferred_element_type=_F32)


def _sigmoid(x):
    return 0.5 + 0.5 * jnp.tanh(0.5 * x)


def _silu(x):
    hx = 0.5 * x
    return hx + hx * jnp.tanh(hx)


def _softplus(x):
    return jnp.maximum(x, 0.0) + jnp.log1p(jnp.exp(-jnp.abs(x)))


def _block_diag(wide, block_mask):
    return jnp.where(block_mask, jnp.concatenate([wide] * N_CHUNKS, axis=0), jnp.zeros((), wide.dtype))


def _layer_kernel(x_ref, ng_ref, win_ref, convw_ref, alog_ref, dtb_ref, dng_ref, wur_ref, wud_ref,
                  wo_ref, fg_ref, cos_ref, sin_ref, dmat_ref, qdec_ref, kdec_ref, cdec_ref,
                  out_ref, proj_ref, cin_ref, g_ref, hb_ref, or_ref, od_ref, nmat_ref, tinv_ref,
                  attn_ref, rhs_ref, w_ref, u_ref, qd_ref, kd_ref, rstate_ref, dstate_ref, *, final_norm):
    hist = pl.ds(SUBLANES, TILE)

    @pl.when(pl.program_id(1) == 0)
    def _():
        cin_ref[:, 0:SUBLANES, :] = jnp.zeros((CONV_BLOCKS, SUBLANES, LANES), _F32)
        rstate_ref[...] = jnp.zeros_like(rstate_ref)
        dstate_ref[...] = jnp.zeros_like(dstate_ref)

    x = x_ref[...]
    h = x * lax.rsqrt(jnp.mean(x * x, axis=-1, keepdims=True) + EPS) * ng_ref[...]
    hb_ref[...] = h.astype(_BF16)

    def project(w_c0, width=PROJ_CHUNK):
        return jnp.dot(hb_ref[...], win_ref[:, w_c0:w_c0 + width], preferred_element_type=_F32)

    def project_ret(c0):
        proj_ref[:, c0:c0 + PROJ_CHUNK] = project(c0)

    def project_conv(blk0):
        res = project(C_DQ + blk0 * LANES)
        for j in range(PROJ_CHUNK // LANES):
            cin_ref[blk0 + j, hist, :] = res[:, j * LANES:(j + 1) * LANES]

    def project_dz(c0):
        proj_ref[:, P_DZ + c0:P_DZ + c0 + PROJ_CHUNK] = project(C_DZ + c0)

    def project_g(c0, width=PROJ_CHUNK):
        g_ref[:, c0:c0 + width] = project(C_G + c0, width)

    project_ret(C_RQ)
    project_ret(C_RK)
    cos = cos_ref[...]
    sin = sin_ref[...]
    qs, ks, scores = [], [], []
    for hh in range(RET_HEADS):
        q = proj_ref[:, C_RQ + hh * RET_DK:C_RQ + (hh + 1) * RET_DK]
        k = proj_ref[:, C_RK + hh * RET_DK:C_RK + (hh + 1) * RET_DK]
        q = q * cos + pltpu.roll(q, RET_DK // 2, 1) * sin
        k = (k * cos + pltpu.roll(k, RET_DK // 2, 1) * sin) * (RET_DK ** -0.5)
        scores.append(_dot_nt(q, k) * dmat_ref[hh])
        qs.append(q)
        ks.append(k)
    project_ret(C_RV)
    project_ret(C_RV + PROJ_CHUNK)
    outs = []
    for hh in range(RET_HEADS):
        v = proj_ref[:, C_RV + hh * RET_DV:C_RV + (hh + 1) * RET_DV]
        outs.append(_dot(jnp.concatenate([scores[hh], qs[hh] * qdec_ref[hh]], axis=1),
                         jnp.concatenate([v, rstate_ref[hh]], axis=0)))
    project_ret(C_RZ)
    project_ret(C_RZ + PROJ_CHUNK)
    for hh in range(RET_HEADS):
        v = proj_ref[:, C_RV + hh * RET_DV:C_RV + (hh + 1) * RET_DV]
        rstate_ref[hh] = rstate_ref[hh] * cdec_ref[hh] + _dot_tn(ks[hh] * kdec_ref[hh], v)
    project_g(0)
    for blk0 in (0, DN_HEADS, 2 * DN_HEADS):
        project_conv(blk0)
    for hh in range(RET_HEADS):
        o = outs[hh]
        z = proj_ref[:, C_RZ + hh * RET_DV:C_RZ + (hh + 1) * RET_DV]
        mu = jnp.mean(o, axis=-1, keepdims=True)
        d = o - mu
        var = jnp.mean(d * d, axis=-1, keepdims=True)
        or_ref[:, hh * RET_DV:(hh + 1) * RET_DV] = (d * lax.rsqrt(var + EPS) * _silu(z)).astype(_BF16)

    bd = g_ref[:, 0:LANES]
    beta_all = _sigmoid(bd)
    g_all = -jnp.exp(alog_ref[...]) * _softplus(bd + dtb_ref[...])
    row_in_chunk = lax.broadcasted_iota(jnp.int32, (TILE, LANES), 0) & (DN_CHUNK - 1)
    gc_all = g_all
    shift = 1
    while shift < DN_CHUNK:
        gc_all = gc_all + jnp.where(row_in_chunk >= shift, pltpu.roll(gc_all, shift, 0), 0.0)
        shift *= 2
    gl_all = jnp.broadcast_to(
        gc_all.reshape(N_CHUNKS, DN_CHUNK, LANES)[:, DN_CHUNK - 1:DN_CHUNK, :],
        (N_CHUNKS, DN_CHUNK, LANES)).reshape(TILE, LANES)
    gc_t = gc_all.T
    e_gl_all = jnp.exp(gl_all)

    ri = lax.broadcasted_iota(jnp.int32, (TILE, TILE), 0)
    ci = lax.broadcasted_iota(jnp.int32, (TILE, TILE), 1)
    same_chunk = (ri // DN_CHUNK) == (ci // DN_CHUNK)
    wr = lax.broadcasted_iota(jnp.int32, (DN_CHUNK, TILE), 0)
    wc = lax.broadcasted_iota(jnp.int32, (DN_CHUNK, TILE), 1) & (DN_CHUNK - 1)
    causal_w = wr >= wc
    strict_w = wr > wc
    eye_wide = (wr == wc).astype(_F32)
    first_in_block = lax.broadcasted_iota(jnp.int32, (DN_CHUNK, LANES), 1) < DN_CHUNK

    def to_wide(full):
        per_block = LANES // DN_CHUNK
        cols = []
        for j in range(TILE // LANES):
            l0 = j * LANES if full.shape[1] == TILE else 0
            parts = [full[(per_block * j + i) * DN_CHUNK:(per_block * j + i + 1) * DN_CHUNK, l0:l0 + LANES]
                     for i in range(per_block)]
            cols.append(jnp.where(first_in_block, parts[0], parts[1]))
        return jnp.concatenate(cols, axis=1)

    cw = convw_ref[...]

    def conv_silu(blk):
        c0 = blk * LANES
        y = (cw[3:4, c0:c0 + LANES] * cin_ref[blk, pl.ds(SUBLANES, TILE), :]
             + cw[2:3, c0:c0 + LANES] * cin_ref[blk, pl.ds(SUBLANES - 1, TILE), :]
             + cw[1:2, c0:c0 + LANES] * cin_ref[blk, pl.ds(SUBLANES - 2, TILE), :]
             + cw[0:1, c0:c0 + LANES] * cin_ref[blk, pl.ds(SUBLANES - 3, TILE), :])
        return _silu(y)

    filler = [
        lambda: project_conv(RET_HEADS),
        lambda: project_conv(DN_HEADS + RET_HEADS),
        lambda: project_conv(2 * DN_HEADS + RET_HEADS),
        lambda: project_dz(0),
        lambda: project_dz(PROJ_CHUNK),
        lambda: project_g(PROJ_CHUNK),
        lambda: project_g(2 * PROJ_CHUNK),
        lambda: None,
    ]

    for hh in range(DN_HEADS):
        q = conv_silu(hh)
        k = conv_silu(DN_HEADS + hh)
        v = conv_silu(2 * DN_HEADS + hh)
        q = q * lax.rsqrt(jnp.sum(q * q, axis=-1, keepdims=True) + EPS) * (DN_DK ** -0.5)
        k = k * lax.rsqrt(jnp.sum(k * k, axis=-1, keepdims=True) + EPS)
        beta = beta_all[:, hh:hh + 1]
        gc = gc_all[:, A_LANE + hh:A_LANE + hh + 1]
        gl = gl_all[:, A_LANE + hh:A_LANE + hh + 1]
        gc_row = gc_t[A_LANE + hh:A_LANE + hh + 1, :]
        gc_rows = to_wide(jnp.broadcast_to(gc, (TILE, LANES)))
        decay = jnp.where(causal_w, jnp.exp(jnp.where(causal_w, gc_rows - gc_row, 0.0)), 0.0)
        k_beta = k * beta
        gram = _dot_nt(jnp.concatenate([k_beta, q], axis=0), k)
        lmat = jnp.where(strict_w, to_wide(gram[:TILE]) * decay, 0.0)
        attn_ref[hh] = (to_wide(gram[TILE:]) * decay).astype(_BF16)
        nmat_ref[hh] = (-lmat).astype(_BF16)
        tinv_ref[hh] = eye_wide - lmat
        e_gc = jnp.exp(gc)
        rhs_ref[hh] = jnp.concatenate([v * beta, k_beta * e_gc], axis=1).astype(_BF16)
        qd_ref[hh] = (q * e_gc).astype(_BF16)
        kd_ref[hh] = (k * jnp.exp(gl - gc)).astype(_BF16)
        filler[hh]()

    cin_ref[:, 0:SUBLANES, :] = cin_ref[:, TILE:TILE + SUBLANES, :]

    for hh in range(DN_HEADS):
        npow = nmat_ref[hh]
        nmat_ref[hh] = jnp.dot(npow, _block_diag(npow, same_chunk), preferred_element_type=_F32).astype(_BF16)
    step = 2
    while step < DN_CHUNK:
        last = step * 2 >= DN_CHUNK
        for hh in range(DN_HEADS):
            npow = nmat_ref[hh]
            tinv = tinv_ref[hh]
            lhs = tinv.astype(_BF16) if last else jnp.concatenate([tinv.astype(_BF16), npow], axis=0)
            prod = jnp.dot(lhs, _block_diag(npow, same_chunk), preferred_element_type=_F32)
            tinv_ref[hh] = tinv + prod[:DN_CHUNK]
            if not last:
                nmat_ref[hh] = prod[DN_CHUNK:].astype(_BF16)
        if step == 2:
            project_g(3 * PROJ_CHUNK)
        if step == 4:
            project_g(4 * PROJ_CHUNK, G_WIDTH - 4 * PROJ_CHUNK)
        step *= 2

    for hh in range(DN_HEADS):
        sol = jnp.dot(_block_diag(tinv_ref[hh].astype(_BF16), same_chunk), rhs_ref[hh],
                      preferred_element_type=_F32)
        u_ref[hh] = sol[:, :DN_DV]
        w_ref[hh] = sol[:, DN_DV:].astype(_BF16)

    y_ret = []
    for c in range(N_CHUNKS):
        r0 = c * DN_CHUNK
        l0 = (r0 // LANES) * LANES
        wss = []
        for hh in range(DN_HEADS):
            wq = jnp.concatenate([w_ref[hh, r0:r0 + DN_CHUNK, :], qd_ref[hh, r0:r0 + DN_CHUNK, :]], axis=0)
            wss.append(jnp.dot(wq, dstate_ref[hh].astype(_BF16), preferred_element_type=_F32))
        v_new = [(u_ref[hh, r0:r0 + DN_CHUNK, :] - wss[hh][:DN_CHUNK]).astype(_BF16) for hh in range(DN_HEADS)]
        for hh in range(DN_HEADS):
            attn_cc = attn_ref[hh, :, l0:l0 + LANES][:, r0 - l0:r0 - l0 + DN_CHUNK]
            u_ref[hh, r0:r0 + DN_CHUNK, :] = wss[hh][DN_CHUNK:] + jnp.dot(
                attn_cc, v_new[hh], preferred_element_type=_F32)
        for hh in range(DN_HEADS):
            e_gl = e_gl_all[r0:r0 + 1, A_LANE + hh:A_LANE + hh + 1]
            dstate_ref[hh] = dstate_ref[hh] * e_gl + lax.dot_general(
                kd_ref[hh, r0:r0 + DN_CHUNK, :], v_new[hh], (((0,), (0,)), ((), ())),
                preferred_element_type=_F32)
        if c * PROJ_CHUNK < D_MODEL:
            c0 = c * PROJ_CHUNK
            y_ret.append(_sigmoid(g_ref[:, G_GR + c0:G_GR + c0 + PROJ_CHUNK]) * jnp.dot(
                or_ref[...], wur_ref[:, c0:c0 + PROJ_CHUNK], preferred_element_type=_F32))

    for hh in range(DN_HEADS):
        o = u_ref[hh]
        o = o * lax.rsqrt(jnp.mean(o * o, axis=-1, keepdims=True) + EPS) * dng_ref[...]
        z = proj_ref[:, P_DZ + hh * DN_DV:P_DZ + (hh + 1) * DN_DV]
        od_ref[:, hh * DN_DV:(hh + 1) * DN_DV] = (o * _silu(z)).astype(_BF16)

    ys = []
    for i, c0 in enumerate(range(0, D_MODEL, PROJ_CHUNK)):
        gate_d = _sigmoid(g_ref[:, G_GD + c0:G_GD + c0 + PROJ_CHUNK])
        ys.append((y_ret[i] + gate_d * jnp.dot(od_ref[...], wud_ref[:, c0:c0 + PROJ_CHUNK],
                                               preferred_element_type=_F32)).astype(_BF16))
    y = jnp.concatenate(ys, axis=1)
    out = x_ref[...] + jnp.dot(y, wo_ref[...], preferred_element_type=_F32)
    if final_norm:
        out = out * lax.rsqrt(jnp.mean(out * out, axis=-1, keepdims=True) + EPS) * fg_ref[...]
    out_ref[...] = out


def _const_spec(shape):
    zeros = (0,) * len(shape)
    return pl.BlockSpec(shape, lambda b, t: zeros, pipeline_mode=pl.Buffered(1))


def _layer_spec(shape, layer):
    idx = (layer,) + (0,) * (len(shape) - 1)
    return pl.BlockSpec((None,) + tuple(shape[1:]), lambda b, t: idx, pipeline_mode=pl.Buffered(1))


def _layer(x, layer, ng, win, convw, alog, dtb, dng, wur, wud, wo, fg, tables, *, final_norm):
    batch, seq, _ = x.shape
    cos, sin, dmat, qdec, kdec, cdec = tables
    stacked = (ng, win, convw, alog, dtb, dng, wur, wud, wo)
    tile_spec = pl.BlockSpec((None, TILE, D_MODEL), lambda b, t: (b, t, 0))
    rope_spec = pl.BlockSpec((TILE, LANES), lambda b, t: (t, 0))
    return pl.pallas_call(
        functools.partial(_layer_kernel, final_norm=final_norm),
        grid=(batch, seq // TILE),
        in_specs=[tile_spec] + [_layer_spec(c.shape, layer) for c in stacked] + [_const_spec(fg.shape)]
        + [rope_spec, rope_spec] + [_const_spec(c.shape) for c in (dmat, qdec, kdec, cdec)],
        out_specs=tile_spec,
        out_shape=jax.ShapeDtypeStruct(x.shape, x.dtype),
        scratch_shapes=[
            pltpu.VMEM((TILE, P_WIDTH), _F32),
            pltpu.VMEM((CONV_BLOCKS, TILE + SUBLANES, LANES), _F32),
            pltpu.VMEM((TILE, G_WIDTH), _F32),
            pltpu.VMEM((TILE, D_MODEL), _BF16),
            pltpu.VMEM((TILE, RET_HEADS * RET_DV), _BF16),
            pltpu.VMEM((TILE, DN_HEADS * DN_DV), _BF16),
            pltpu.VMEM((DN_HEADS, DN_CHUNK, TILE), _BF16),
            pltpu.VMEM((DN_HEADS, DN_CHUNK, TILE), _F32),
            pltpu.VMEM((DN_HEADS, DN_CHUNK, TILE), _BF16),
            pltpu.VMEM((DN_HEADS, TILE, DN_DV + DN_DK), _BF16),
            pltpu.VMEM((DN_HEADS, TILE, DN_DK), _BF16),
            pltpu.VMEM((DN_HEADS, TILE, DN_DV), _F32),
            pltpu.VMEM((DN_HEADS, TILE, DN_DK), _BF16),
            pltpu.VMEM((DN_HEADS, TILE, DN_DK), _BF16),
            pltpu.VMEM((RET_HEADS, RET_DK, RET_DV), _F32),
            pltpu.VMEM((DN_HEADS, DN_DK, DN_DV), _F32),
        ],
        compiler_params=pltpu.CompilerParams(
            dimension_semantics=("arbitrary", "arbitrary"),
            vmem_limit_bytes=VMEM_LIMIT_BYTES),
        name="hybrid_layer",
    )(x, *stacked, fg, cos, sin, dmat, qdec, kdec, cdec)


def _tables(seq):
    half = RET_DK // 2
    inv_freq = np.power(ROPE_BASE, -np.arange(half, dtype=np.float64) / half)
    ang = np.arange(seq, dtype=np.float64)[:, None] * inv_freq[None, :]
    cos = np.concatenate([np.cos(ang), np.cos(ang)], axis=1)
    sin = np.concatenate([-np.sin(ang), np.sin(ang)], axis=1)
    log_g = np.log1p(-np.power(2.0, -5.0 - np.arange(RET_HEADS, dtype=np.float64)))
    idx = np.arange(TILE, dtype=np.float64)
    rel = idx[:, None] - idx[None, :]
    dmat = np.where(rel[None] >= 0, np.exp(np.maximum(rel, 0.0)[None] * log_g[:, None, None]), 0.0)
    qdec = np.broadcast_to(np.exp((idx + 1.0)[None, :] * log_g[:, None])[:, :, None],
                           (RET_HEADS, TILE, RET_DK))
    kdec = np.broadcast_to(np.exp((TILE - 1 - idx)[None, :] * log_g[:, None])[:, :, None],
                           (RET_HEADS, TILE, RET_DK))
    cdec = np.broadcast_to(np.exp(TILE * log_g)[:, None, None], (RET_HEADS, 1, RET_DV))
    return tuple(jnp.asarray(np.ascontiguousarray(t), dtype=_F32) for t in (cos, sin, dmat, qdec, kdec, cdec))


def _lane_rows(vals, first_lane):
    depth, n = vals.shape
    return jnp.zeros((depth, 1, LANES), _F32).at[:, 0, first_lane:first_lane + n].set(vals.astype(_F32))


def kernel(x, norm_gain, w_in, conv_w, a_log, dt_bias, dn_norm_gain, w_up_ret, w_up_dn, w_out, final_gain):
    assert x.shape[1] % TILE == 0 and x.shape[2] == D_MODEL and w_in.shape[2] == IN_W
    tables = _tables(x.shape[1])
    params = (norm_gain[:, None, :], jnp.pad(w_in.astype(_BF16), ((0, 0), (0, 0), (0, N_PROJ - IN_W))),
              conv_w, _lane_rows(a_log, A_LANE), _lane_rows(dt_bias, A_LANE), dn_norm_gain[:, None, :],
              w_up_ret.astype(_BF16), w_up_dn.astype(_BF16), w_out.astype(_BF16))
    for l in range(DEPTH):
        x = _layer(x, l, *params, final_gain[None, :], tables, final_norm=(l == DEPTH - 1))
    return x
```

```python
import functools

import numpy as np

import jax
import jax.numpy as jnp
from jax import lax
from jax.experimental import pallas as pl
from jax.experimental.pallas import tpu as pltpu

D_MODEL = 1024
DEPTH = 2
RET_HEADS = 4
RET_DK = 128
RET_DV = 256
DN_HEADS = 8
DN_DK = 128
DN_DV = 128
DN_CHUNK = 64
CONV_K = 4
ROPE_BASE = 10000.0
EPS = 1e-6

LANES = 128
SUBLANES = 8
SEQS = 2
HALF = 128
TILE = SEQS * HALF
N_CHUNKS = TILE // DN_CHUNK
SEQ_CHUNKS = HALF // DN_CHUNK

C_RQ = 0
C_RK = C_RQ + RET_HEADS * RET_DK
C_RV = C_RK + RET_HEADS * RET_DK
C_RZ = C_RV + RET_HEADS * RET_DV
C_DQ = C_RZ + RET_HEADS * RET_DV
C_DK = C_DQ + DN_HEADS * DN_DK
C_DV = C_DK + DN_HEADS * DN_DK
C_DZ = C_DV + DN_HEADS * DN_DV
C_G = C_DZ + DN_HEADS * DN_DV
G_GR = 2 * DN_HEADS
G_GD = G_GR + D_MODEL
IN_W = C_G + G_GD + D_MODEL
PROJ_CHUNK = 512
N_PROJ = -(-IN_W // LANES) * LANES
G_WIDTH = N_PROJ - C_G
A_LANE = DN_HEADS
P_DZ = C_DQ
P_WIDTH = P_DZ + DN_HEADS * DN_DV
CONV_BLOCKS = (C_DZ - C_DQ) // LANES

VMEM_LIMIT_BYTES = 56 * 1024 * 1024

_F32 = jnp.float32
_BF16 = jnp.bfloat16


def _dot(a, b):
    return jnp.dot(a.astype(_BF16), b.astype(_BF16), preferred_element_type=_F32)


def _dot_nt(a, b):
    return lax.dot_general(a.astype(_BF16), b.astype(_BF16), (((1,), (1,)), ((), ())),
                           preferred_element_type=_F32)


def _dot_tn(a, b):
    return lax.dot_general(a.astype(_BF16), b.astype(_BF16), (((0,), (0,)), ((), ())),
                           preferred_element_type=_F32)


def _sigmoid(x):
    return 0.5 + 0.5 * jnp.tanh(0.5 * x)


def _silu(x):
    hx = 0.5 * x
    return hx + hx * jnp.tanh(hx)


def _softplus(x):
    return jnp.maximum(x, 0.0) + jnp.log1p(jnp.exp(-jnp.abs(x)))


def _block_diag(wide, block_mask):
    return jnp.where(block_mask, jnp.concatenate([wide] * N_CHUNKS, axis=0), jnp.zeros((), wide.dtype))


def _layer_kernel(x_ref, ng_ref, win_ref, convw_ref, alog_ref, dtb_ref, dng_ref, wur_ref, wud_ref,
                  wo_ref, fg_ref, cos_ref, sin_ref, dmat_ref, qdec_ref, kdec_ref, cdec_ref,
                  out_ref, proj_ref, cin_ref, g_ref, hb_ref, or_ref, od_ref, nmat_ref, tinv_ref,
                  attn_ref, rhs_ref, w_ref, u_ref, qd_ref, kd_ref, rstate_ref, dstate_ref, *, final_norm):
    hist = pl.ds(SUBLANES, HALF)

    @pl.when(pl.program_id(1) == 0)
    def _():
        cin_ref[:, :, 0:SUBLANES, :] = jnp.zeros((CONV_BLOCKS, SEQS, SUBLANES, LANES), _F32)
        rstate_ref[...] = jnp.zeros_like(rstate_ref)
        dstate_ref[...] = jnp.zeros_like(dstate_ref)

    x = x_ref[...].reshape(TILE, D_MODEL)
    h = x * lax.rsqrt(jnp.mean(x * x, axis=-1, keepdims=True) + EPS) * ng_ref[...]
    hb_ref[...] = h.astype(_BF16)

    def project(w_c0, width=PROJ_CHUNK):
        return jnp.dot(hb_ref[...], win_ref[:, w_c0:w_c0 + width], preferred_element_type=_F32)

    def project_ret(c0):
        proj_ref[:, c0:c0 + PROJ_CHUNK] = project(c0)

    def project_conv(blk0):
        res = project(C_DQ + blk0 * LANES)
        for j in range(PROJ_CHUNK // LANES):
            for s in range(SEQS):
                cin_ref[blk0 + j, s, hist, :] = res[s * HALF:(s + 1) * HALF, j * LANES:(j + 1) * LANES]

    def project_dz(c0):
        proj_ref[:, P_DZ + c0:P_DZ + c0 + PROJ_CHUNK] = project(C_DZ + c0)

    def project_g(c0, width=PROJ_CHUNK):
        g_ref[:, c0:c0 + width] = project(C_G + c0, width)

    project_ret(C_RQ)
    project_ret(C_RK)
    cos = jnp.concatenate([cos_ref[...]] * SEQS, axis=0)
    sin = jnp.concatenate([sin_ref[...]] * SEQS, axis=0)
    seq_of_row = lax.broadcasted_iota(jnp.int32, (TILE, RET_DK), 0) // HALF
    qs, ks, scores = [], [], []
    for hh in range(RET_HEADS):
        q = proj_ref[:, C_RQ + hh * RET_DK:C_RQ + (hh + 1) * RET_DK]
        k = proj_ref[:, C_RK + hh * RET_DK:C_RK + (hh + 1) * RET_DK]
        q = q * cos + pltpu.roll(q, RET_DK // 2, 1) * sin
        k = (k * cos + pltpu.roll(k, RET_DK // 2, 1) * sin) * (RET_DK ** -0.5)
        scores.append(_dot_nt(q, k) * dmat_ref[hh])
        qs.append(q)
        ks.append(k)
    project_ret(C_RV)
    project_ret(C_RV + PROJ_CHUNK)
    outs = []
    for hh in range(RET_HEADS):
        v = proj_ref[:, C_RV + hh * RET_DV:C_RV + (hh + 1) * RET_DV]
        q_dec = qs[hh] * qdec_ref[hh]
        outs.append(_dot(
            jnp.concatenate([scores[hh]] + [jnp.where(seq_of_row == s, q_dec, 0.0) for s in range(SEQS)], axis=1),
            jnp.concatenate([v] + [rstate_ref[s, hh] for s in range(SEQS)], axis=0)))
    project_ret(C_RZ)
    project_ret(C_RZ + PROJ_CHUNK)
    for hh in range(RET_HEADS):
        v = proj_ref[:, C_RV + hh * RET_DV:C_RV + (hh + 1) * RET_DV]
        k_dec = ks[hh] * kdec_ref[hh]
        for s in range(SEQS):
            rows_s = slice(s * HALF, (s + 1) * HALF)
            rstate_ref[s, hh] = rstate_ref[s, hh] * cdec_ref[hh] + _dot_tn(k_dec[rows_s], v[rows_s])
    project_g(0)
    for blk0 in (0, DN_HEADS, 2 * DN_HEADS):
        project_conv(blk0)
    for hh in range(RET_HEADS):
        o = outs[hh]
        z = proj_ref[:, C_RZ + hh * RET_DV:C_RZ + (hh + 1) * RET_DV]
        mu = jnp.mean(o, axis=-1, keepdims=True)
        d = o - mu
        var = jnp.mean(d * d, axis=-1, keepdims=True)
        or_ref[:, hh * RET_DV:(hh + 1) * RET_DV] = (d * lax.rsqrt(var + EPS) * _silu(z)).astype(_BF16)

    bd = g_ref[:, 0:LANES]
    beta_all = _sigmoid(bd)
    g_all = -jnp.exp(alog_ref[...]) * _softplus(bd + dtb_ref[...])
    row_in_chunk = lax.broadcasted_iota(jnp.int32, (TILE, LANES), 0) & (DN_CHUNK - 1)
    gc_all = g_all
    shift = 1
    while shift < DN_CHUNK:
        gc_all = gc_all + jnp.where(row_in_chunk >= shift, pltpu.roll(gc_all, shift, 0), 0.0)
        shift *= 2
    gl_all = jnp.broadcast_to(
        gc_all.reshape(N_CHUNKS, DN_CHUNK, LANES)[:, DN_CHUNK - 1:DN_CHUNK, :],
        (N_CHUNKS, DN_CHUNK, LANES)).reshape(TILE, LANES)
    gc_t = gc_all.T
    e_gl_all = jnp.exp(gl_all)

    ri = lax.broadcasted_iota(jnp.int32, (TILE, TILE), 0)
    ci = lax.broadcasted_iota(jnp.int32, (TILE, TILE), 1)
    same_chunk = (ri // DN_CHUNK) == (ci // DN_CHUNK)
    wr = lax.broadcasted_iota(jnp.int32, (DN_CHUNK, TILE), 0)
    wc = lax.broadcasted_iota(jnp.int32, (DN_CHUNK, TILE), 1) & (DN_CHUNK - 1)
    causal_w = wr >= wc
    strict_w = wr > wc
    eye_wide = (wr == wc).astype(_F32)
    first_in_block = lax.broadcasted_iota(jnp.int32, (DN_CHUNK, LANES), 1) < DN_CHUNK

    def to_wide(full):
        per_block = LANES // DN_CHUNK
        cols = []
        for j in range(TILE // LANES):
            l0 = j * LANES if full.shape[1] == TILE else 0
            parts = [full[(per_block * j + i) * DN_CHUNK:(per_block * j + i + 1) * DN_CHUNK, l0:l0 + LANES]
                     for i in range(per_block)]
            cols.append(jnp.where(first_in_block, parts[0], parts[1]))
        return jnp.concatenate(cols, axis=1)

    cw = convw_ref[...]

    def conv_silu(blk):
        c0 = blk * LANES
        ys = [cw[3:4, c0:c0 + LANES] * cin_ref[blk, s, pl.ds(SUBLANES, HALF), :]
              + cw[2:3, c0:c0 + LANES] * cin_ref[blk, s, pl.ds(SUBLANES - 1, HALF), :]
              + cw[1:2, c0:c0 + LANES] * cin_ref[blk, s, pl.ds(SUBLANES - 2, HALF), :]
              + cw[0:1, c0:c0 + LANES] * cin_ref[blk, s, pl.ds(SUBLANES - 3, HALF), :]
              for s in range(SEQS)]
        return _silu(jnp.concatenate(ys, axis=0))

    filler = [
        lambda: project_conv(RET_HEADS),
        lambda: project_conv(DN_HEADS + RET_HEADS),
        lambda: project_conv(2 * DN_HEADS + RET_HEADS),
        lambda: project_dz(0),
        lambda: project_dz(PROJ_CHUNK),
        lambda: project_g(PROJ_CHUNK),
        lambda: project_g(2 * PROJ_CHUNK),
        lambda: None,
    ]

    for hh in range(DN_HEADS):
        q = conv_silu(hh)
        k = conv_silu(DN_HEADS + hh)
        v = conv_silu(2 * DN_HEADS + hh)
        q = q * lax.rsqrt(jnp.sum(q * q, axis=-1, keepdims=True) + EPS) * (DN_DK ** -0.5)
        k = k * lax.rsqrt(jnp.sum(k * k, axis=-1, keepdims=True) + EPS)
        beta = beta_all[:, hh:hh + 1]
        gc = gc_all[:, A_LANE + hh:A_LANE + hh + 1]
        gl = gl_all[:, A_LANE + hh:A_LANE + hh + 1]
        gc_row = gc_t[A_LANE + hh:A_LANE + hh + 1, :]
        gc_rows = to_wide(jnp.broadcast_to(gc, (TILE, LANES)))
        decay = jnp.where(causal_w, jnp.exp(jnp.where(causal_w, gc_rows - gc_row, 0.0)), 0.0)
        k_beta = k * beta
        gram = _dot_nt(jnp.concatenate([k_beta, q], axis=0), k)
        lmat = jnp.where(strict_w, to_wide(gram[:TILE]) * decay, 0.0)
        attn_ref[hh] = (to_wide(gram[TILE:]) * decay).astype(_BF16)
        nmat_ref[hh] = (-lmat).astype(_BF16)
        tinv_ref[hh] = eye_wide - lmat
        e_gc = jnp.exp(gc)
        rhs_ref[hh] = jnp.concatenate([v * beta, k_beta * e_gc], axis=1).astype(_BF16)
        qd_ref[hh] = (q * e_gc).astype(_BF16)
        kd_ref[hh] = (k * jnp.exp(gl - gc)).astype(_BF16)
        filler[hh]()

    cin_ref[:, :, 0:SUBLANES, :] = cin_ref[:, :, HALF:HALF + SUBLANES, :]

    for hh in range(DN_HEADS):
        npow = nmat_ref[hh]
        nmat_ref[hh] = jnp.dot(npow, _block_diag(npow, same_chunk), preferred_element_type=_F32).astype(_BF16)
    step = 2
    while step < DN_CHUNK:
        last = step * 2 >= DN_CHUNK
        for hh in range(DN_HEADS):
            npow = nmat_ref[hh]
            tinv = tinv_ref[hh]
            lhs = tinv.astype(_BF16) if last else jnp.concatenate([tinv.astype(_BF16), npow], axis=0)
            prod = jnp.dot(lhs, _block_diag(npow, same_chunk), preferred_element_type=_F32)
            tinv_ref[hh] = tinv + prod[:DN_CHUNK]
            if not last:
                nmat_ref[hh] = prod[DN_CHUNK:].astype(_BF16)
        if step == 2:
            project_g(3 * PROJ_CHUNK)
        if step == 4:
            project_g(4 * PROJ_CHUNK, G_WIDTH - 4 * PROJ_CHUNK)
        step *= 2

    for hh in range(DN_HEADS):
        sol = jnp.dot(_block_diag(tinv_ref[hh].astype(_BF16), same_chunk), rhs_ref[hh],
                      preferred_element_type=_F32)
        u_ref[hh] = sol[:, :DN_DV]
        w_ref[hh] = sol[:, DN_DV:].astype(_BF16)

    y_ret = []
    for step in range(SEQ_CHUNKS):
        work = [(s * SEQ_CHUNKS + step, s, hh) for s in range(SEQS) for hh in range(DN_HEADS)]
        wss = {}
        for c, s, hh in work:
            r0 = c * DN_CHUNK
            wq = jnp.concatenate([w_ref[hh, r0:r0 + DN_CHUNK, :], qd_ref[hh, r0:r0 + DN_CHUNK, :]], axis=0)
            wss[c, hh] = jnp.dot(wq, dstate_ref[s, hh].astype(_BF16), preferred_element_type=_F32)
        v_new = {(c, hh): (u_ref[hh, c * DN_CHUNK:(c + 1) * DN_CHUNK, :] - wss[c, hh][:DN_CHUNK]).astype(_BF16)
                 for c, s, hh in work}
        for c, s, hh in work:
            r0 = c * DN_CHUNK
            l0 = (r0 // LANES) * LANES
            attn_cc = attn_ref[hh, :, l0:l0 + LANES][:, r0 - l0:r0 - l0 + DN_CHUNK]
            u_ref[hh, r0:r0 + DN_CHUNK, :] = wss[c, hh][DN_CHUNK:] + jnp.dot(
                attn_cc, v_new[c, hh], preferred_element_type=_F32)
        for c, s, hh in work:
            r0 = c * DN_CHUNK
            e_gl = e_gl_all[r0:r0 + 1, A_LANE + hh:A_LANE + hh + 1]
            dstate_ref[s, hh] = dstate_ref[s, hh] * e_gl + lax.dot_general(
                kd_ref[hh, r0:r0 + DN_CHUNK, :], v_new[c, hh], (((0,), (0,)), ((), ())),
                preferred_element_type=_F32)
        if step * PROJ_CHUNK < D_MODEL:
            c0 = step * PROJ_CHUNK
            y_ret.append(_sigmoid(g_ref[:, G_GR + c0:G_GR + c0 + PROJ_CHUNK]) * jnp.dot(
                or_ref[...], wur_ref[:, c0:c0 + PROJ_CHUNK], preferred_element_type=_F32))

    for hh in range(DN_HEADS):
        o = u_ref[hh]
        o = o * lax.rsqrt(jnp.mean(o * o, axis=-1, keepdims=True) + EPS) * dng_ref[...]
        z = proj_ref[:, P_DZ + hh * DN_DV:P_DZ + (hh + 1) * DN_DV]
        od_ref[:, hh * DN_DV:(hh + 1) * DN_DV] = (o * _silu(z)).astype(_BF16)

    ys = []
    for i, c0 in enumerate(range(0, D_MODEL, PROJ_CHUNK)):
        gate_d = _sigmoid(g_ref[:, G_GD + c0:G_GD + c0 + PROJ_CHUNK])
        ys.append((y_ret[i] + gate_d * jnp.dot(od_ref[...], wud_ref[:, c0:c0 + PROJ_CHUNK],
                                               preferred_element_type=_F32)).astype(_BF16))
    y = jnp.concatenate(ys, axis=1)
    out = x_ref[...].reshape(TILE, D_MODEL) + jnp.dot(y, wo_ref[...], preferred_element_type=_F32)
    if final_norm:
        out = out * lax.rsqrt(jnp.mean(out * out, axis=-1, keepdims=True) + EPS) * fg_ref[...]
    out_ref[...] = out.reshape(SEQS, HALF, D_MODEL)


def _const_spec(shape):
    zeros = (0,) * len(shape)
    return pl.BlockSpec(shape, lambda b, t: zeros, pipeline_mode=pl.Buffered(1))


def _layer_spec(shape, layer):
    idx = (layer,) + (0,) * (len(shape) - 1)
    return pl.BlockSpec((None,) + tuple(shape[1:]), lambda b, t: idx, pipeline_mode=pl.Buffered(1))


def _layer(x, layer, ng, win, convw, alog, dtb, dng, wur, wud, wo, fg, tables, *, final_norm):
    batch, seq, _ = x.shape
    cos, sin, dmat, qdec, kdec, cdec = tables
    stacked = (ng, win, convw, alog, dtb, dng, wur, wud, wo)
    tile_spec = pl.BlockSpec((SEQS, HALF, D_MODEL), lambda b, t: (b, t, 0))
    rope_spec = pl.BlockSpec((HALF, LANES), lambda b, t: (t, 0))
    return pl.pallas_call(
        functools.partial(_layer_kernel, final_norm=final_norm),
        grid=(batch // SEQS, seq // HALF),
        in_specs=[tile_spec] + [_layer_spec(c.shape, layer) for c in stacked] + [_const_spec(fg.shape)]
        + [rope_spec, rope_spec] + [_const_spec(c.shape) for c in (dmat, qdec, kdec, cdec)],
        out_specs=tile_spec,
        out_shape=jax.ShapeDtypeStruct(x.shape, x.dtype),
        scratch_shapes=[
            pltpu.VMEM((TILE, P_WIDTH), _F32),
            pltpu.VMEM((CONV_BLOCKS, SEQS, HALF + SUBLANES, LANES), _F32),
            pltpu.VMEM((TILE, G_WIDTH), _F32),
            pltpu.VMEM((TILE, D_MODEL), _BF16),
            pltpu.VMEM((TILE, RET_HEADS * RET_DV), _BF16),
            pltpu.VMEM((TILE, DN_HEADS * DN_DV), _BF16),
            pltpu.VMEM((DN_HEADS, DN_CHUNK, TILE), _BF16),
            pltpu.VMEM((DN_HEADS, DN_CHUNK, TILE), _F32),
            pltpu.VMEM((DN_HEADS, DN_CHUNK, TILE), _BF16),
            pltpu.VMEM((DN_HEADS, TILE, DN_DV + DN_DK), _BF16),
            pltpu.VMEM((DN_HEADS, TILE, DN_DK), _BF16),
            pltpu.VMEM((DN_HEADS, TILE, DN_DV), _F32),
            pltpu.VMEM((DN_HEADS, TILE, DN_DK), _BF16),
            pltpu.VMEM((DN_HEADS, TILE, DN_DK), _BF16),
            pltpu.VMEM((SEQS, RET_HEADS, RET_DK, RET_DV), _F32),
            pltpu.VMEM((SEQS, DN_HEADS, DN_DK, DN_DV), _F32),
        ],
        compiler_params=pltpu.CompilerParams(
            dimension_semantics=("arbitrary", "arbitrary"),
            vmem_limit_bytes=VMEM_LIMIT_BYTES),
        name="hybrid_layer",
    )(x, *stacked, fg, cos, sin, dmat, qdec, kdec, cdec)


def _tables(seq):
    half = RET_DK // 2
    inv_freq = np.power(ROPE_BASE, -np.arange(half, dtype=np.float64) / half)
    ang = np.arange(seq, dtype=np.float64)[:, None] * inv_freq[None, :]
    cos = np.concatenate([np.cos(ang), np.cos(ang)], axis=1)
    sin = np.concatenate([-np.sin(ang), np.sin(ang)], axis=1)
    log_g = np.log1p(-np.power(2.0, -5.0 - np.arange(RET_HEADS, dtype=np.float64)))
    idx = np.arange(TILE, dtype=np.float64) % HALF
    seq = np.arange(TILE) // HALF
    rel = idx[:, None] - idx[None, :]
    same_seq = seq[:, None] == seq[None, :]
    dmat = np.where((rel[None] >= 0) & same_seq[None],
                    np.exp(np.maximum(rel, 0.0)[None] * log_g[:, None, None]), 0.0)
    qdec = np.broadcast_to(np.exp((idx + 1.0)[None, :] * log_g[:, None])[:, :, None],
                           (RET_HEADS, TILE, RET_DK))
    kdec = np.broadcast_to(np.exp((HALF - 1 - idx)[None, :] * log_g[:, None])[:, :, None],
                           (RET_HEADS, TILE, RET_DK))
    cdec = np.broadcast_to(np.exp(HALF * log_g)[:, None, None], (RET_HEADS, 1, RET_DV))
    return tuple(jnp.asarray(np.ascontiguousarray(t), dtype=_F32) for t in (cos, sin, dmat, qdec, kdec, cdec))


def _lane_rows(vals, first_lane):
    depth, n = vals.shape
    return jnp.zeros((depth, 1, LANES), _F32).at[:, 0, first_lane:first_lane + n].set(vals.astype(_F32))


def kernel(x, norm_gain, w_in, conv_w, a_log, dt_bias, dn_norm_gain, w_up_ret, w_up_dn, w_out, final_gain):
    assert x.shape[0] % SEQS == 0 and x.shape[1] % HALF == 0 and x.shape[2] == D_MODEL and w_in.shape[2] == IN_W
    tables = _tables(x.shape[1])
    params = (norm_gain[:, None, :], jnp.pad(w_in.astype(_BF16), ((0, 0), (0, 0), (0, N_PROJ - IN_W))),
              conv_w, _lane_rows(a_log, A_LANE), _lane_rows(dt_bias, A_LANE), dn_norm_gain[:, None, :],
              w_up_ret.astype(_BF16), w_up_dn.astype(_BF16), w_out.astype(_BF16))
    for l in range(DEPTH):
        x = _layer(x, l, *params, final_gain[None, :], tables, final_norm=(l == DEPTH - 1))
    return x
```

```python
import functools

import numpy as np

import jax
import jax.numpy as jnp
from jax import lax
from jax.experimental import pallas as pl
from jax.experimental.pallas import tpu as pltpu

D_MODEL = 1024
DEPTH = 2
RET_HEADS = 4
RET_DK = 128
RET_DV = 256
DN_HEADS = 8
DN_DK = 128
DN_DV = 128
DN_CHUNK = 64
CONV_K = 4
ROPE_BASE = 10000.0
EPS = 1e-6

LANES = 128
SUBLANES = 8
SEQS = 2
HALF = 128
TILE = SEQS * HALF
N_CHUNKS = TILE // DN_CHUNK
SEQ_CHUNKS = HALF // DN_CHUNK

C_RQ = 0
C_RK = C_RQ + RET_HEADS * RET_DK
C_RV = C_RK + RET_HEADS * RET_DK
C_RZ = C_RV + RET_HEADS * RET_DV
C_DQ = C_RZ + RET_HEADS * RET_DV
C_DK = C_DQ + DN_HEADS * DN_DK
C_DV = C_DK + DN_HEADS * DN_DK
C_DZ = C_DV + DN_HEADS * DN_DV
C_G = C_DZ + DN_HEADS * DN_DV
G_GR = 2 * DN_HEADS
G_GD = G_GR + D_MODEL
IN_W = C_G + G_GD + D_MODEL
PROJ_CHUNK = 512
N_PROJ = -(-IN_W // LANES) * LANES
G_WIDTH = N_PROJ - C_G
A_LANE = DN_HEADS
P_DZ = C_DQ
P_WIDTH = P_DZ + DN_HEADS * DN_DV
CONV_BLOCKS = (C_DZ - C_DQ) // LANES

VMEM_LIMIT_BYTES = 56 * 1024 * 1024

_F32 = jnp.float32
_BF16 = jnp.bfloat16


def _dot(a, b):
    return jnp.dot(a.astype(_BF16), b.astype(_BF16), preferred_element_type=_F32)


def _dot_nt(a, b):
    return lax.dot_general(a.astype(_BF16), b.astype(_BF16), (((1,), (1,)), ((), ())),
                           preferred_element_type=_F32)


def _dot_tn(a, b):
    return lax.dot_general(a.astype(_BF16), b.astype(_BF16), (((0,), (0,)), ((), ())),
                           preferred_element_type=_F32)


def _sigmoid(x):
    return 0.5 + 0.5 * jnp.tanh(0.5 * x)


def _silu(x):
    hx = 0.5 * x
    return hx + hx * jnp.tanh(hx)


def _softplus(x):
    return jnp.maximum(x, 0.0) + jnp.log1p(jnp.exp(-jnp.abs(x)))


def _block_diag(wide, block_mask):
    return jnp.where(block_mask, jnp.concatenate([wide] * N_CHUNKS, axis=0), jnp.zeros((), wide.dtype))


def _layer_kernel(x_ref, ng_ref, win_ref, convw_ref, alog_ref, dtb_ref, dng_ref, wur_ref, wud_ref,
                  wo_ref, fg_ref, cos_ref, sin_ref, dmat_ref, qdec_ref, kdec_ref, cdec_ref,
                  out_ref, proj_ref, cin_ref, g_ref, hb_ref, or_ref, od_ref, nmat_ref, tinv_ref,
                  attn_ref, rhs_ref, w_ref, u_ref, qd_ref, kd_ref, rstate_ref, dstate_ref, *, final_norm):
    hist = pl.ds(SUBLANES, HALF)

    @pl.when(pl.program_id(1) == 0)
    def _():
        cin_ref[:, :, 0:SUBLANES, :] = jnp.zeros((CONV_BLOCKS, SEQS, SUBLANES, LANES), _F32)
        rstate_ref[...] = jnp.zeros_like(rstate_ref)
        dstate_ref[...] = jnp.zeros_like(dstate_ref)

    x = x_ref[...].reshape(TILE, D_MODEL)
    h = x * lax.rsqrt(jnp.mean(x * x, axis=-1, keepdims=True) + EPS) * ng_ref[...]
    hb_ref[...] = h.astype(_BF16)

    def project(w_c0, width=PROJ_CHUNK):
        return jnp.dot(hb_ref[...], win_ref[:, w_c0:w_c0 + width], preferred_element_type=_F32)

    def project_ret(c0):
        proj_ref[:, c0:c0 + PROJ_CHUNK] = project(c0)

    def project_conv(blk0):
        res = project(C_DQ + blk0 * LANES)
        for j in range(PROJ_CHUNK // LANES):
            for s in range(SEQS):
                cin_ref[blk0 + j, s, hist, :] = res[s * HALF:(s + 1) * HALF, j * LANES:(j + 1) * LANES]

    def project_dz(c0):
        proj_ref[:, P_DZ + c0:P_DZ + c0 + PROJ_CHUNK] = project(C_DZ + c0)

    def project_g(c0, width=PROJ_CHUNK):
        g_ref[:, c0:c0 + width] = project(C_G + c0, width)

    project_ret(C_RQ)
    project_ret(C_RK)
    cos = jnp.concatenate([cos_ref[...]] * SEQS, axis=0)
    sin = jnp.concatenate([sin_ref[...]] * SEQS, axis=0)
    qs, ks, scores = [], [], []
    for hh in range(RET_HEADS):
        q = proj_ref[:, C_RQ + hh * RET_DK:C_RQ + (hh + 1) * RET_DK]
        k = proj_ref[:, C_RK + hh * RET_DK:C_RK + (hh + 1) * RET_DK]
        q = q * cos + pltpu.roll(q, RET_DK // 2, 1) * sin
        k = (k * cos + pltpu.roll(k, RET_DK // 2, 1) * sin) * (RET_DK ** -0.5)
        scores.append([_dot_nt(q[s * HALF:(s + 1) * HALF], k[s * HALF:(s + 1) * HALF]) * dmat_ref[hh]
                       for s in range(SEQS)])
        qs.append(q)
        ks.append(k)
    project_ret(C_RV)
    project_ret(C_RV + PROJ_CHUNK)
    outs = []
    for hh in range(RET_HEADS):
        v = proj_ref[:, C_RV + hh * RET_DV:C_RV + (hh + 1) * RET_DV]
        q_dec = qs[hh] * qdec_ref[hh]
        outs.append(jnp.concatenate(
            [_dot(jnp.concatenate([scores[hh][s], q_dec[s * HALF:(s + 1) * HALF]], axis=1),
                  jnp.concatenate([v[s * HALF:(s + 1) * HALF], rstate_ref[s, hh]], axis=0))
             for s in range(SEQS)], axis=0))
    project_ret(C_RZ)
    project_ret(C_RZ + PROJ_CHUNK)
    for hh in range(RET_HEADS):
        v = proj_ref[:, C_RV + hh * RET_DV:C_RV + (hh + 1) * RET_DV]
        k_dec = ks[hh] * kdec_ref[hh]
        for s in range(SEQS):
            rows_s = slice(s * HALF, (s + 1) * HALF)
            rstate_ref[s, hh] = rstate_ref[s, hh] * cdec_ref[hh] + _dot_tn(k_dec[rows_s], v[rows_s])
    project_g(0)
    for blk0 in (0, DN_HEADS, 2 * DN_HEADS):
        project_conv(blk0)
    for hh in range(RET_HEADS):
        o = outs[hh]
        z = proj_ref[:, C_RZ + hh * RET_DV:C_RZ + (hh + 1) * RET_DV]
        mu = jnp.mean(o, axis=-1, keepdims=True)
        d = o - mu
        var = jnp.mean(d * d, axis=-1, keepdims=True)
        or_ref[:, hh * RET_DV:(hh + 1) * RET_DV] = (d * lax.rsqrt(var + EPS) * _silu(z)).astype(_BF16)

    bd = g_ref[:, 0:LANES]
    beta_all = _sigmoid(bd)
    g_all = -jnp.exp(alog_ref[...]) * _softplus(bd + dtb_ref[...])
    row_in_chunk = lax.broadcasted_iota(jnp.int32, (TILE, LANES), 0) & (DN_CHUNK - 1)
    gc_all = g_all
    shift = 1
    while shift < DN_CHUNK:
        gc_all = gc_all + jnp.where(row_in_chunk >= shift, pltpu.roll(gc_all, shift, 0), 0.0)
        shift *= 2
    gl_all = jnp.broadcast_to(
        gc_all.reshape(N_CHUNKS, DN_CHUNK, LANES)[:, DN_CHUNK - 1:DN_CHUNK, :],
        (N_CHUNKS, DN_CHUNK, LANES)).reshape(TILE, LANES)
    gc_t = gc_all.T
    e_gl_all = jnp.exp(gl_all)

    ri = lax.broadcasted_iota(jnp.int32, (TILE, TILE), 0)
    ci = lax.broadcasted_iota(jnp.int32, (TILE, TILE), 1)
    same_chunk = (ri // DN_CHUNK) == (ci // DN_CHUNK)
    wr = lax.broadcasted_iota(jnp.int32, (DN_CHUNK, TILE), 0)
    wc = lax.broadcasted_iota(jnp.int32, (DN_CHUNK, TILE), 1) & (DN_CHUNK - 1)
    causal_w = wr >= wc
    strict_w = wr > wc
    eye_wide = (wr == wc).astype(_F32)
    first_in_block = lax.broadcasted_iota(jnp.int32, (DN_CHUNK, LANES), 1) < DN_CHUNK

    def to_wide(full):
        per_block = LANES // DN_CHUNK
        cols = []
        for j in range(TILE // LANES):
            l0 = j * LANES if full.shape[1] == TILE else 0
            parts = [full[(per_block * j + i) * DN_CHUNK:(per_block * j + i + 1) * DN_CHUNK, l0:l0 + LANES]
                     for i in range(per_block)]
            cols.append(jnp.where(first_in_block, parts[0], parts[1]))
        return jnp.concatenate(cols, axis=1)

    cw = convw_ref[...]

    def conv_silu(blk):
        c0 = blk * LANES
        ys = [cw[3:4, c0:c0 + LANES] * cin_ref[blk, s, pl.ds(SUBLANES, HALF), :]
              + cw[2:3, c0:c0 + LANES] * cin_ref[blk, s, pl.ds(SUBLANES - 1, HALF), :]
              + cw[1:2, c0:c0 + LANES] * cin_ref[blk, s, pl.ds(SUBLANES - 2, HALF), :]
              + cw[0:1, c0:c0 + LANES] * cin_ref[blk, s, pl.ds(SUBLANES - 3, HALF), :]
              for s in range(SEQS)]
        return _silu(jnp.concatenate(ys, axis=0))

    filler = [
        lambda: project_conv(RET_HEADS),
        lambda: project_conv(DN_HEADS + RET_HEADS),
        lambda: project_conv(2 * DN_HEADS + RET_HEADS),
        lambda: project_dz(0),
        lambda: project_dz(PROJ_CHUNK),
        lambda: project_g(PROJ_CHUNK),
        lambda: project_g(2 * PROJ_CHUNK),
        lambda: None,
    ]

    for hh in range(DN_HEADS):
        q = conv_silu(hh)
        k = conv_silu(DN_HEADS + hh)
        v = conv_silu(2 * DN_HEADS + hh)
        q = q * lax.rsqrt(jnp.sum(q * q, axis=-1, keepdims=True) + EPS) * (DN_DK ** -0.5)
        k = k * lax.rsqrt(jnp.sum(k * k, axis=-1, keepdims=True) + EPS)
        beta = beta_all[:, hh:hh + 1]
        gc = gc_all[:, A_LANE + hh:A_LANE + hh + 1]
        gl = gl_all[:, A_LANE + hh:A_LANE + hh + 1]
        gc_row = gc_t[A_LANE + hh:A_LANE + hh + 1, :]
        gc_rows = to_wide(jnp.broadcast_to(gc, (TILE, LANES)))
        decay = jnp.where(causal_w, jnp.exp(jnp.where(causal_w, gc_rows - gc_row, 0.0)), 0.0)
        k_beta = k * beta
        gram = _dot_nt(jnp.concatenate([k_beta, q], axis=0), k)
        lmat = jnp.where(strict_w, to_wide(gram[:TILE]) * decay, 0.0)
        attn_ref[hh] = (to_wide(gram[TILE:]) * decay).astype(_BF16)
        nmat_ref[hh] = (-lmat).astype(_BF16)
        tinv_ref[hh] = eye_wide - lmat
        e_gc = jnp.exp(gc)
        rhs_ref[hh] = jnp.concatenate([v * beta, k_beta * e_gc], axis=1).astype(_BF16)
        qd_ref[hh] = (q * e_gc).astype(_BF16)
        kd_ref[hh] = (k * jnp.exp(gl - gc)).astype(_BF16)
        filler[hh]()

    cin_ref[:, :, 0:SUBLANES, :] = cin_ref[:, :, HALF:HALF + SUBLANES, :]

    for hh in range(DN_HEADS):
        npow = nmat_ref[hh]
        nmat_ref[hh] = jnp.dot(npow, _block_diag(npow, same_chunk), preferred_element_type=_F32).astype(_BF16)
    step = 2
    while step < DN_CHUNK:
        last = step * 2 >= DN_CHUNK
        for hh in range(DN_HEADS):
            npow = nmat_ref[hh]
            tinv = tinv_ref[hh]
            lhs = tinv.astype(_BF16) if last else jnp.concatenate([tinv.astype(_BF16), npow], axis=0)
            prod = jnp.dot(lhs, _block_diag(npow, same_chunk), preferred_element_type=_F32)
            tinv_ref[hh] = tinv + prod[:DN_CHUNK]
            if not last:
                nmat_ref[hh] = prod[DN_CHUNK:].astype(_BF16)
        if step == 2:
            project_g(3 * PROJ_CHUNK)
        if step == 4:
            project_g(4 * PROJ_CHUNK, G_WIDTH - 4 * PROJ_CHUNK)
        step *= 2

    for hh in range(DN_HEADS):
        sol = jnp.dot(_block_diag(tinv_ref[hh].astype(_BF16), same_chunk), rhs_ref[hh],
                      preferred_element_type=_F32)
        u_ref[hh] = sol[:, :DN_DV]
        w_ref[hh] = sol[:, DN_DV:].astype(_BF16)

    y_ret = []
    for step in range(SEQ_CHUNKS):
        work = [(s * SEQ_CHUNKS + step, s, hh) for s in range(SEQS) for hh in range(DN_HEADS)]
        wss = {}
        for c, s, hh in work:
            r0 = c * DN_CHUNK
            wq = jnp.concatenate([w_ref[hh, r0:r0 + DN_CHUNK, :], qd_ref[hh, r0:r0 + DN_CHUNK, :]], axis=0)
            wss[c, hh] = jnp.dot(wq, dstate_ref[s, hh].astype(_BF16), preferred_element_type=_F32)
        v_new = {(c, hh): (u_ref[hh, c * DN_CHUNK:(c + 1) * DN_CHUNK, :] - wss[c, hh][:DN_CHUNK]).astype(_BF16)
                 for c, s, hh in work}
        for c, s, hh in work:
            r0 = c * DN_CHUNK
            l0 = (r0 // LANES) * LANES
            attn_cc = attn_ref[hh, :, l0:l0 + LANES][:, r0 - l0:r0 - l0 + DN_CHUNK]
            u_ref[hh, r0:r0 + DN_CHUNK, :] = wss[c, hh][DN_CHUNK:] + jnp.dot(
                attn_cc, v_new[c, hh], preferred_element_type=_F32)
        for c, s, hh in work:
            r0 = c * DN_CHUNK
            e_gl = e_gl_all[r0:r0 + 1, A_LANE + hh:A_LANE + hh + 1]
            dstate_ref[s, hh] = dstate_ref[s, hh] * e_gl + lax.dot_general(
                kd_ref[hh, r0:r0 + DN_CHUNK, :], v_new[c, hh], (((0,), (0,)), ((), ())),
                preferred_element_type=_F32)
        if step * PROJ_CHUNK < D_MODEL:
            c0 = step * PROJ_CHUNK
            y_ret.append(_sigmoid(g_ref[:, G_GR + c0:G_GR + c0 + PROJ_CHUNK]) * jnp.dot(
                or_ref[...], wur_ref[:, c0:c0 + PROJ_CHUNK], preferred_element_type=_F32))

    for hh in range(DN_HEADS):
        o = u_ref[hh]
        o = o * lax.rsqrt(jnp.mean(o * o, axis=-1, keepdims=True) + EPS) * dng_ref[...]
        z = proj_ref[:, P_DZ + hh * DN_DV:P_DZ + (hh + 1) * DN_DV]
        od_ref[:, hh * DN_DV:(hh + 1) * DN_DV] = (o * _silu(z)).astype(_BF16)

    ys = []
    for i, c0 in enumerate(range(0, D_MODEL, PROJ_CHUNK)):
        gate_d = _sigmoid(g_ref[:, G_GD + c0:G_GD + c0 + PROJ_CHUNK])
        ys.append((y_ret[i] + gate_d * jnp.dot(od_ref[...], wud_ref[:, c0:c0 + PROJ_CHUNK],
                                               preferred_element_type=_F32)).astype(_BF16))
    y = jnp.concatenate(ys, axis=1)
    out = x_ref[...].reshape(TILE, D_MODEL) + jnp.dot(y, wo_ref[...], preferred_element_type=_F32)
    if final_norm:
        out = out * lax.rsqrt(jnp.mean(out * out, axis=-1, keepdims=True) + EPS) * fg_ref[...]
    out_ref[...] = out.reshape(SEQS, HALF, D_MODEL)


def _const_spec(shape):
    zeros = (0,) * len(shape)
    return pl.BlockSpec(shape, lambda b, t: zeros, pipeline_mode=pl.Buffered(1))


def _layer_spec(shape, layer):
    idx = (layer,) + (0,) * (len(shape) - 1)
    return pl.BlockSpec((None,) + tuple(shape[1:]), lambda b, t: idx, pipeline_mode=pl.Buffered(1))


def _layer(x, layer, ng, win, convw, alog, dtb, dng, wur, wud, wo, fg, tables, *, final_norm):
    batch, seq, _ = x.shape
    cos, sin, dmat, qdec, kdec, cdec = tables
    stacked = (ng, win, convw, alog, dtb, dng, wur, wud, wo)
    tile_spec = pl.BlockSpec((SEQS, HALF, D_MODEL), lambda b, t: (b, t, 0))
    rope_spec = pl.BlockSpec((HALF, LANES), lambda b, t: (t, 0))
    return pl.pallas_call(
        functools.partial(_layer_kernel, final_norm=final_norm),
        grid=(batch // SEQS, seq // HALF),
        in_specs=[tile_spec] + [_layer_spec(c.shape, layer) for c in stacked] + [_const_spec(fg.shape)]
        + [rope_spec, rope_spec] + [_const_spec(c.shape) for c in (dmat, qdec, kdec, cdec)],
        out_specs=tile_spec,
        out_shape=jax.ShapeDtypeStruct(x.shape, x.dtype),
        scratch_shapes=[
            pltpu.VMEM((TILE, P_WIDTH), _F32),
            pltpu.VMEM((CONV_BLOCKS, SEQS, HALF + SUBLANES, LANES), _F32),
            pltpu.VMEM((TILE, G_WIDTH), _F32),
            pltpu.VMEM((TILE, D_MODEL), _BF16),
            pltpu.VMEM((TILE, RET_HEADS * RET_DV), _BF16),
            pltpu.VMEM((TILE, DN_HEADS * DN_DV), _BF16),
            pltpu.VMEM((DN_HEADS, DN_CHUNK, TILE), _BF16),
            pltpu.VMEM((DN_HEADS, DN_CHUNK, TILE), _F32),
            pltpu.VMEM((DN_HEADS, DN_CHUNK, TILE), _BF16),
            pltpu.VMEM((DN_HEADS, TILE, DN_DV + DN_DK), _BF16),
            pltpu.VMEM((DN_HEADS, TILE, DN_DK), _BF16),
            pltpu.VMEM((DN_HEADS, TILE, DN_DV), _F32),
            pltpu.VMEM((DN_HEADS, TILE, DN_DK), _BF16),
            pltpu.VMEM((DN_HEADS, TILE, DN_DK), _BF16),
            pltpu.VMEM((SEQS, RET_HEADS, RET_DK, RET_DV), _F32),
            pltpu.VMEM((SEQS, DN_HEADS, DN_DK, DN_DV), _F32),
        ],
        compiler_params=pltpu.CompilerParams(
            dimension_semantics=("arbitrary", "arbitrary"),
            vmem_limit_bytes=VMEM_LIMIT_BYTES),
        name="hybrid_layer",
    )(x, *stacked, fg, cos, sin, dmat, qdec, kdec, cdec)


def _tables(seq):
    half = RET_DK // 2
    inv_freq = np.power(ROPE_BASE, -np.arange(half, dtype=np.float64) / half)
    ang = np.arange(seq, dtype=np.float64)[:, None] * inv_freq[None, :]
    cos = np.concatenate([np.cos(ang), np.cos(ang)], axis=1)
    sin = np.concatenate([-np.sin(ang), np.sin(ang)], axis=1)
    log_g = np.log1p(-np.power(2.0, -5.0 - np.arange(RET_HEADS, dtype=np.float64)))
    idx = np.arange(TILE, dtype=np.float64) % HALF
    rel = idx[:, None] - idx[None, :]
    rel = rel[:HALF, :HALF]
    dmat = np.where(rel[None] >= 0, np.exp(np.maximum(rel, 0.0)[None] * log_g[:, None, None]), 0.0)
    qdec = np.broadcast_to(np.exp((idx + 1.0)[None, :] * log_g[:, None])[:, :, None],
                           (RET_HEADS, TILE, RET_DK))
    kdec = np.broadcast_to(np.exp((HALF - 1 - idx)[None, :] * log_g[:, None])[:, :, None],
                           (RET_HEADS, TILE, RET_DK))
    cdec = np.broadcast_to(np.exp(HALF * log_g)[:, None, None], (RET_HEADS, 1, RET_DV))
    return tuple(jnp.asarray(np.ascontiguousarray(t), dtype=_F32) for t in (cos, sin, dmat, qdec, kdec, cdec))


def _lane_rows(vals, first_lane):
    depth, n = vals.shape
    return jnp.zeros((depth, 1, LANES), _F32).at[:, 0, first_lane:first_lane + n].set(vals.astype(_F32))


def kernel(x, norm_gain, w_in, conv_w, a_log, dt_bias, dn_norm_gain, w_up_ret, w_up_dn, w_out, final_gain):
    assert x.shape[0] % SEQS == 0 and x.shape[1] % HALF == 0 and x.shape[2] == D_MODEL and w_in.shape[2] == IN_W
    tables = _tables(x.shape[1])
    params = (norm_gain[:, None, :], jnp.pad(w_in.astype(_BF16), ((0, 0), (0, 0), (0, N_PROJ - IN_W))),
              conv_w, _lane_rows(a_log, A_LANE), _lane_rows(dt_bias, A_LANE), dn_norm_gain[:, None, :],
              w_up_ret.astype(_BF16), w_up_dn.astype(_BF16), w_out.astype(_BF16))
    for l in range(DEPTH):
        x = _layer(x, l, *params, final_gain[None, :], tables, final_norm=(l == DEPTH - 1))
    return x
```
